```python
import math
import jax
import jax.numpy as jnp
from jax import lax
import numpy as np

D_MODEL = 2048
BATCH = 8
SEQ = 4096
DEPTH = 4

CTX_LEN = 256
GRID_W = 64
HEAD_DIM = 128
N_MIXERS = 4
MIX_WIDTH = D_MODEL
GROUP_HEADS = MIX_WIDTH // (HEAD_DIM * N_MIXERS)
CHUNK = 64
Q_BLOCK = 128
ROPE_BASE = 10000.0
NORM_EPS = 1e-6

GDN_HEADS = GROUP_HEADS
GDN_DK = HEAD_DIM
GDN_DV = HEAD_DIM
CONV_K = 3
GQA_HEADS = GROUP_HEADS
GQA_KV_HEADS = GROUP_HEADS // 2
ML_HEADS = GROUP_HEADS
ML_DQK = HEAD_DIM // 2
ML_DV = HEAD_DIM
FORGET_BIAS = 3.0
MLA_HEADS = GROUP_HEADS
MLA_Q_RANK = 384
MLA_KV_RANK = 128
MLA_NOPE = 128
MLA_ROPE = 64
MLA_DV = 128
FF_DIM = 4 * D_MODEL

GDN_COLS = (GDN_HEADS * (2 * GDN_DK + GDN_DV), GDN_HEADS * GDN_DV, 2 * GDN_HEADS, 2 * GDN_HEADS)
GQA_COLS = (GQA_HEADS * HEAD_DIM, GQA_KV_HEADS * HEAD_DIM, GQA_KV_HEADS * HEAD_DIM)
ML_COLS = (ML_HEADS * ML_DQK, ML_HEADS * ML_DQK, ML_HEADS * ML_DV, ML_HEADS * ML_DV, 2 * ML_HEADS, 2 * ML_HEADS)
MLA_COLS = (MLA_Q_RANK, MLA_KV_RANK, MLA_ROPE)
GROUP_COLS = (sum(GDN_COLS), sum(GQA_COLS), sum(ML_COLS), sum(MLA_COLS))
IN_WIDTH = sum(GROUP_COLS)

kernel_name = 'hybrid_parallel_group_diffusion_trunk'


def split_cols(y, sizes):
    parts, start = [], 0
    for s in sizes:
        parts.append(y[..., start:start + s])
        start += s
    return parts


def rmsnorm(x, g):
    xf = x.astype(jnp.float32)
    y = xf * lax.rsqrt(jnp.mean(xf * xf, axis=-1, keepdims=True) + NORM_EPS)
    return y.astype(x.dtype) * g


def l2norm(x):
    xf = x.astype(jnp.float32)
    return (xf * lax.rsqrt(jnp.sum(xf * xf, axis=-1, keepdims=True) + NORM_EPS)).astype(x.dtype)


def modulate(h, shift, scale):
    return h * (1.0 + scale) + shift


def axial_rope(x, pos):
    row, col = pos
    half = x.shape[-1] // 2
    quarter = half // 2
    inv_freq = ROPE_BASE ** (-jnp.arange(quarter, dtype=jnp.float32) / quarter)

    def rotate(xp, p):
        ang = p.astype(jnp.float32)[:, None] * inv_freq
        cos, sin = jnp.cos(ang)[None, :, None, :], jnp.sin(ang)[None, :, None, :]
        x1 = xp[..., :quarter].astype(jnp.float32)
        x2 = xp[..., quarter:].astype(jnp.float32)
        return jnp.concatenate([x1 * cos - x2 * sin, x2 * cos + x1 * sin], axis=-1)

    return jnp.concatenate([rotate(x[..., :half], row), rotate(x[..., half:], col)], axis=-1).astype(x.dtype)


def centred_conv(x, w):
    k, ch = w.shape
    return lax.conv_general_dilated(x, w[:, None, :].astype(x.dtype), window_strides=(1,),
                                    padding=[(k // 2, k // 2)], dimension_numbers=('NWC', 'WIO', 'NWC'),
                                    feature_group_count=ch)


def block_attention(q, k, v):
    b, lq, h, dq = q.shape
    kvh, dv = k.shape[2], v.shape[-1]
    g = h // kvh
    nb = lq // Q_BLOCK
    scale = dq ** -0.5
    qb = jnp.moveaxis(q.reshape(b, nb, Q_BLOCK, kvh, g, dq), 1, 0)

    def attend(q_blk):
        s = jnp.einsum('bqhgd,bkhd->bhgqk', q_blk, k).astype(jnp.float32) * scale
        p = jax.nn.softmax(s, axis=-1).astype(v.dtype)
        return jnp.einsum('bhgqk,bkhd->bqhgd', p, v)

    o = lax.map(attend, qb)
    return jnp.moveaxis(o, 0, 1).reshape(b, lq, h, dv)


def _chunk(t):
    b, h, l = t.shape[:3]
    return jnp.moveaxis(t.reshape(b, h, l // CHUNK, CHUNK, *t.shape[3:]), 2, 0)


def _unchunk(t):
    n, b, h, c = t.shape[:4]
    return jnp.moveaxis(t, 0, 2).reshape(b, h, n * c, *t.shape[4:])


def _ident(t):
    return t


def _flip(t):
    return jnp.flip(t, axis=2)


def gdn_scan(q, k, v, log_a, beta, state):
    out_dtype = v.dtype
    q, k, v, log_a, beta = (_chunk(t.astype(jnp.float32)) for t in (q, k, v, log_a, beta))
    g = jnp.cumsum(log_a, axis=-1)
    idx = jnp.arange(CHUNK)
    incl = idx[:, None] >= idx[None, :]
    decay = jnp.exp(jnp.where(incl, g[..., :, None] - g[..., None, :], -jnp.inf))
    kb = k * beta[..., None]
    a_low = jnp.where(idx[:, None] > idx[None, :], jnp.einsum('nbhid,nbhjd->nbhij', kb, k) * decay, 0.0)
    eye = jnp.eye(CHUNK, dtype=jnp.float32)
    t_inv = lax.linalg.triangular_solve(eye + a_low, jnp.broadcast_to(eye, a_low.shape), left_side=True,
                                        lower=True, unit_diagonal=True)
    vv = t_inv @ (v * beta[..., None])
    ww = t_inv @ (kb * jnp.exp(g)[..., None])
    attn = jnp.einsum('nbhid,nbhjd->nbhij', q, k) * decay

    def step(s, xs):
        q_c, k_c, vv_c, ww_c, attn_c, g_c = xs
        v_new = vv_c - ww_c @ s
        o = (q_c * jnp.exp(g_c)[..., None]) @ s + attn_c @ v_new
        g_end = g_c[..., -1]
        s = s * jnp.exp(g_end)[..., None, None] + jnp.einsum(
            'bhcd,bhce->bhde', k_c * jnp.exp(g_end[..., None] - g_c)[..., None], v_new)
        return s, o

    state, o = lax.scan(step, state.astype(jnp.float32), (q, k, vv, ww, attn, g))
    return _unchunk(o).astype(out_dtype), state


def mlstm_scan(q, k, v, i_pre, f_pre, state):
    out_dtype = v.dtype
    q, k, v, i_pre, f_pre = (_chunk(t.astype(jnp.float32)) for t in (q, k, v, i_pre, f_pre))
    b = jnp.cumsum(jax.nn.log_sigmoid(f_pre), axis=-1)
    idx = jnp.arange(CHUNK)
    incl = idx[:, None] >= idx[None, :]
    dmat = jnp.where(incl, b[..., :, None] - b[..., None, :] + i_pre[..., None, :], -jnp.inf)
    qk = jnp.einsum('nbhid,nbhjd->nbhij', q, k)
    w_end = b[..., -1:] - b + i_pre

    def step(carry, xs):
        cmat, nvec, m = carry
        q_c, k_c, v_c, b_c, d_c, qk_c, w_c = xs
        inter = b_c + m[..., None]
        m_t = jnp.maximum(inter, d_c.max(-1))
        s_inter = jnp.exp(inter - m_t)
        p = jnp.exp(d_c - m_t[..., None]) * qk_c
        num = s_inter[..., None] * (q_c @ cmat) + p @ v_c
        den = s_inter * jnp.einsum('bhcd,bhd->bhc', q_c, nvec) + p.sum(-1)
        h = num / jnp.maximum(jnp.abs(den), jnp.exp(-m_t))[..., None]
        m_new = jnp.maximum(b_c[..., -1] + m, w_c.max(-1))
        s_old = jnp.exp(b_c[..., -1] + m - m_new)
        wk = jnp.exp(w_c - m_new[..., None])[..., None] * k_c
        cmat = s_old[..., None, None] * cmat + jnp.einsum('bhcd,bhce->bhde', wk, v_c)
        nvec = s_old[..., None] * nvec + wk.sum(-2)
        return (cmat, nvec, m_new), h

    state, h = lax.scan(step, state, (q, k, v, b, dmat, qk, w_end))
    return _unchunk(h).astype(out_dtype), state


def gdn_prep(u, conv_w, a_log, dt_bias):
    bsz, l, _ = u.shape
    qkv, gate, alpha, beta = split_cols(u, GDN_COLS)
    qkv = jax.nn.silu(centred_conv(qkv, conv_w))
    q, k, v = split_cols(qkv, (GDN_HEADS * GDN_DK, GDN_HEADS * GDN_DK, GDN_HEADS * GDN_DV))
    q = l2norm(q.reshape(bsz, l, GDN_HEADS, GDN_DK)).transpose(0, 2, 1, 3) * (GDN_DK ** -0.5)
    k = l2norm(k.reshape(bsz, l, GDN_HEADS, GDN_DK)).transpose(0, 2, 1, 3)
    v = v.reshape(bsz, l, GDN_HEADS, GDN_DV).transpose(0, 2, 1, 3)
    log_a = -jnp.exp(a_log) * jax.nn.softplus(alpha.reshape(bsz, l, 2, GDN_HEADS) + dt_bias)
    log_a = log_a.transpose(2, 0, 3, 1)
    beta = jax.nn.sigmoid(beta.reshape(bsz, l, 2, GDN_HEADS)).transpose(2, 0, 3, 1)
    return q, k, v, gate, log_a, beta


def gdn_mixer(u_ctx, u_lat, conv_w, a_log, dt_bias, norm_w, ctx_out):
    pc = gdn_prep(u_ctx, conv_w, a_log, dt_bias)
    pl = gdn_prep(u_lat, conv_w, a_log, dt_bias)
    bsz = u_lat.shape[0]
    o_ctx, o_lat = 0.0, 0.0
    for d in range(2):
        f = _ident if d == 0 else _flip
        s0 = jnp.zeros((bsz, GDN_HEADS, GDN_DK, GDN_DV), jnp.float32)
        oc, s_ctx = gdn_scan(f(pc[0]), f(pc[1]), f(pc[2]), f(pc[4][d]), f(pc[5][d]), s0)
        ol, _ = gdn_scan(f(pl[0]), f(pl[1]), f(pl[2]), f(pl[4][d]), f(pl[5][d]), s_ctx)
        o_lat = o_lat + f(ol)
        if ctx_out:
            o_ctx = o_ctx + f(oc)

    def finish(o, gate):
        l = o.shape[2]
        y = rmsnorm(o.transpose(0, 2, 1, 3), norm_w) * jax.nn.silu(gate.reshape(bsz, l, GDN_HEADS, GDN_DV))
        return y.reshape(bsz, l, GDN_HEADS * GDN_DV)

    return (finish(o_ctx, pc[3]) if ctx_out else None), finish(o_lat, pl[3])


def gqa_mixer(u_ctx, u_lat, qk_norm, pos, ctx_out):
    def prep(u, p):
        bsz, l, _ = u.shape
        q, k, v = split_cols(u, GQA_COLS)
        q = rmsnorm(q.reshape(bsz, l, GQA_HEADS, HEAD_DIM), qk_norm[0])
        k = rmsnorm(k.reshape(bsz, l, GQA_KV_HEADS, HEAD_DIM), qk_norm[1])
        v = v.reshape(bsz, l, GQA_KV_HEADS, HEAD_DIM)
        if p is not None:
            q, k = axial_rope(q, p), axial_rope(k, p)
        return q, k, v

    qc, kc, vc = prep(u_ctx, None)
    ql, kl, vl = prep(u_lat, pos)
    bsz, l = u_lat.shape[:2]
    y_lat = block_attention(ql, jnp.concatenate([kc, kl], axis=1), jnp.concatenate([vc, vl], axis=1))
    y_ctx = block_attention(qc, kc, vc).reshape(bsz, u_ctx.shape[1], -1) if ctx_out else None
    return y_ctx, y_lat.reshape(bsz, l, -1)


def mlstm_prep(u, gate_bias):
    bsz, l, _ = u.shape
    q, k, v, og, ig, fg = split_cols(u, ML_COLS)
    q = q.reshape(bsz, l, ML_HEADS, ML_DQK).transpose(0, 2, 1, 3) * (ML_DQK ** -0.5)
    k = k.reshape(bsz, l, ML_HEADS, ML_DQK).transpose(0, 2, 1, 3)
    v = v.reshape(bsz, l, ML_HEADS, ML_DV).transpose(0, 2, 1, 3)
    ig = (ig.reshape(bsz, l, 2, ML_HEADS) + gate_bias[0]).transpose(2, 0, 3, 1)
    fg = (fg.reshape(bsz, l, 2, ML_HEADS) + gate_bias[1]).transpose(2, 0, 3, 1)
    return q, k, v, og, ig, fg


def mlstm_mixer(u_ctx, u_lat, gate_bias, norm_w, ctx_out):
    pc = mlstm_prep(u_ctx, gate_bias)
    pl = mlstm_prep(u_lat, gate_bias)
    bsz = u_lat.shape[0]
    h_ctx, h_lat = 0.0, 0.0
    for d in range(2):
        f = _ident if d == 0 else _flip
        s0 = (jnp.zeros((bsz, ML_HEADS, ML_DQK, ML_DV), jnp.float32),
              jnp.zeros((bsz, ML_HEADS, ML_DQK), jnp.float32),
              jnp.zeros((bsz, ML_HEADS), jnp.float32))
        hc, s_ctx = mlstm_scan(f(pc[0]), f(pc[1]), f(pc[2]), f(pc[4][d]), f(pc[5][d]), s0)
        hl, _ = mlstm_scan(f(pl[0]), f(pl[1]), f(pl[2]), f(pl[4][d]), f(pl[5][d]), s_ctx)
        h_lat = h_lat + f(hl)
        if ctx_out:
            h_ctx = h_ctx + f(hc)

    def finish(h, og):
        l = h.shape[2]
        y = rmsnorm(h.transpose(0, 2, 1, 3), norm_w) * jax.nn.sigmoid(og.reshape(bsz, l, ML_HEADS, ML_DV))
        return y.reshape(bsz, l, ML_HEADS * ML_DV)

    return (finish(h_ctx, pc[3]) if ctx_out else None), finish(h_lat, pl[3])


def mla_mixer(u_ctx, u_lat, q_norm, kv_norm, w_qb, w_kvb, pos, ctx_out):
    def prep(u, p):
        bsz, l, _ = u.shape
        cq, ckv, k_pe = split_cols(u, MLA_COLS)
        q = (rmsnorm(cq, q_norm) @ w_qb).reshape(bsz, l, MLA_HEADS, MLA_NOPE + MLA_ROPE)
        kv = (rmsnorm(ckv, kv_norm) @ w_kvb).reshape(bsz, l, MLA_HEADS, MLA_NOPE + MLA_DV)
        q_nope, q_pe = q[..., :MLA_NOPE], q[..., MLA_NOPE:]
        k_nope, v = kv[..., :MLA_NOPE], kv[..., MLA_NOPE:]
        k_pe = k_pe[:, :, None, :]
        if p is not None:
            q_pe, k_pe = axial_rope(q_pe, p), axial_rope(k_pe, p)
        q = jnp.concatenate([q_nope, q_pe], axis=-1)
        k = jnp.concatenate([k_nope, jnp.broadcast_to(k_pe, (bsz, l, MLA_HEADS, MLA_ROPE))], axis=-1)
        return q, k, v

    qc, kc, vc = prep(u_ctx, None)
    ql, kl, vl = prep(u_lat, pos)
    bsz, l = u_lat.shape[:2]
    y_lat = block_attention(ql, jnp.concatenate([kc, kl], axis=1), jnp.concatenate([vc, vl], axis=1))
    y_ctx = block_attention(qc, kc, vc).reshape(bsz, u_ctx.shape[1], -1) if ctx_out else None
    return y_ctx, y_lat.reshape(bsz, l, -1)


def sandwich_update(h, y_mix, mod, g4, w_out, w_ff1, w_ff2):
    _, _, gate_mix, shift_ff, scale_ff, gate_ff = mod
    h = h + gate_mix * rmsnorm(y_mix @ w_out, g4[1])
    z = modulate(rmsnorm(h, g4[2]), shift_ff, scale_ff)
    z = jnp.square(jax.nn.relu(z @ w_ff1)) @ w_ff2
    return h + gate_ff * rmsnorm(z, g4[3])


def setup_inputs(seed: int = 0) -> dict:
    key = jax.random.key(seed)
    ks = jax.random.split(key, 24)
    f32 = jnp.float32

    def dense(k, shape, fan_in):
        return jax.random.normal(k, shape, f32) * (fan_in ** -0.5)

    def gain(k, shape):
        return 1.0 + 0.05 * jax.random.normal(k, shape, f32)

    dt = jnp.exp(jax.random.uniform(ks[9], (DEPTH, 2, GDN_HEADS), f32, math.log(1e-3), math.log(1e-1)))
    i_bias = 0.1 * jax.random.normal(ks[12], (DEPTH, 1, 2, ML_HEADS), f32)
    f_bias = FORGET_BIAS + 0.1 * jax.random.normal(ks[13], (DEPTH, 1, 2, ML_HEADS), f32)
    return {
        'x': jax.random.normal(ks[0], (BATCH, SEQ, D_MODEL), f32),
        'c': jax.random.normal(ks[1], (BATCH, D_MODEL), f32),
        'ctx': jax.random.normal(ks[2], (BATCH, CTX_LEN, D_MODEL), f32),
        'c_ctx': jax.random.normal(ks[3], (D_MODEL,), f32),
        'w_mod': dense(ks[4], (DEPTH, D_MODEL, 6 * D_MODEL), D_MODEL),
        'b_mod': 0.02 * jax.random.normal(ks[5], (DEPTH, 6 * D_MODEL), f32),
        'g_norm': gain(ks[6], (DEPTH, 4, D_MODEL)),
        'w_in': dense(ks[7], (DEPTH, D_MODEL, IN_WIDTH), D_MODEL),
        'gdn_conv': dense(ks[8], (DEPTH, CONV_K, GDN_COLS[0]), CONV_K),
        'gdn_a_log': jnp.log(jax.random.uniform(ks[10], (DEPTH, 2, GDN_HEADS), f32, 1.0, 16.0)),
        'gdn_dt_bias': jnp.log(jnp.expm1(dt)),
        'gdn_norm': gain(ks[11], (DEPTH, GDN_DV)),
        'gqa_qk_norm': gain(ks[14], (DEPTH, 2, HEAD_DIM)),
        'mlstm_gate_bias': jnp.concatenate([i_bias, f_bias], axis=1),
        'mlstm_norm': gain(ks[15], (DEPTH, ML_HEADS, ML_DV)),
        'mla_q_norm': gain(ks[16], (DEPTH, MLA_Q_RANK)),
        'mla_kv_norm': gain(ks[17], (DEPTH, MLA_KV_RANK)),
        'mla_w_qb': dense(ks[18], (DEPTH, MLA_Q_RANK, MLA_HEADS * (MLA_NOPE + MLA_ROPE)), MLA_Q_RANK),
        'mla_w_kvb': dense(ks[19], (DEPTH, MLA_KV_RANK, MLA_HEADS * (MLA_NOPE + MLA_DV)), MLA_KV_RANK),
        'w_out': dense(ks[20], (DEPTH, MIX_WIDTH, D_MODEL), MIX_WIDTH),
        'w_ff1': dense(ks[21], (DEPTH, D_MODEL, FF_DIM), D_MODEL),
        'w_ff2': dense(ks[22], (DEPTH, FF_DIM, D_MODEL), FF_DIM),
    }


def reference(x, c, ctx, c_ctx, w_mod, b_mod, g_norm, w_in, gdn_conv, gdn_a_log, gdn_dt_bias, gdn_norm,
              gqa_qk_norm, mlstm_gate_bias, mlstm_norm, mla_q_norm, mla_kv_norm, mla_w_qb, mla_w_kvb,
              w_out, w_ff1, w_ff2):
    seq = x.shape[1]
    rows = seq // GRID_W
    row_pos = jnp.repeat(jnp.arange(rows, dtype=jnp.int32), GRID_W)
    col_pos = jnp.tile(jnp.arange(GRID_W, dtype=jnp.int32), rows)
    pos = (row_pos, col_pos)
    h_lat, h_ctx = x, ctx
    for layer in range(DEPTH):
        ctx_out = layer < DEPTH - 1
        mod_lat = jnp.split((jax.nn.silu(c) @ w_mod[layer] + b_mod[layer])[:, None, :], 6, axis=-1)
        mod_ctx = jnp.split((jax.nn.silu(c_ctx) @ w_mod[layer] + b_mod[layer])[None, None, :], 6, axis=-1)
        g4 = g_norm[layer]
        u_lat = split_cols(modulate(rmsnorm(h_lat, g4[0]), mod_lat[0], mod_lat[1]) @ w_in[layer], GROUP_COLS)
        u_ctx = split_cols(modulate(rmsnorm(h_ctx, g4[0]), mod_ctx[0], mod_ctx[1]) @ w_in[layer], GROUP_COLS)
        ya_ctx, ya_lat = gdn_mixer(u_ctx[0], u_lat[0], gdn_conv[layer], gdn_a_log[layer], gdn_dt_bias[layer],
                                   gdn_norm[layer], ctx_out)
        yb_ctx, yb_lat = gqa_mixer(u_ctx[1], u_lat[1], gqa_qk_norm[layer], pos, ctx_out)
        yc_ctx, yc_lat = mlstm_mixer(u_ctx[2], u_lat[2], mlstm_gate_bias[layer], mlstm_norm[layer], ctx_out)
        yd_ctx, yd_lat = mla_mixer(u_ctx[3], u_lat[3], mla_q_norm[layer], mla_kv_norm[layer], mla_w_qb[layer],
                                   mla_w_kvb[layer], pos, ctx_out)
        y_lat = jnp.concatenate([ya_lat, yb_lat, yc_lat, yd_lat], axis=-1)
        new_lat = sandwich_update(h_lat, y_lat, mod_lat, g4, w_out[layer], w_ff1[layer], w_ff2[layer])
        if ctx_out:
            y_ctx = jnp.concatenate([ya_ctx, yb_ctx, yc_ctx, yd_ctx], axis=-1)
            h_ctx = sandwich_update(h_ctx, y_ctx, mod_ctx, g4, w_out[layer], w_ff1[layer], w_ff2[layer])
        h_lat = new_lat
    return h_lat
```

```python
import functools
import math

import jax
import jax.numpy as jnp
from jax import lax
from jax.experimental import pallas as pl
from jax.experimental.pallas import tpu as pltpu

F32 = jnp.float32
BF16 = jnp.bfloat16

LANE = 128
HEAD_DIM = 128
CHUNK = 64
GRID_W = 64
ROPE_BASE = 10000.0
NORM_EPS = 1e-6
ML_DQK = 64
MLA_Q_RANK = 384
MLA_KV_RANK = 128
MLA_NOPE = 128
MLA_ROPE = 64
MLA_DV = 128
MLA_QK_PAD = 256
VMEM_LIMIT = 56 * 1024 * 1024


def _dot(a, b):
    return jnp.dot(a.astype(BF16), b.astype(BF16), preferred_element_type=F32)


def _dot_nt(a, b):
    return lax.dot_general(a.astype(BF16), b.astype(BF16), (((1,), (1,)), ((), ())),
                           preferred_element_type=F32)


def _dot_tn(a, b):
    return lax.dot_general(a.astype(BF16), b.astype(BF16), (((0,), (0,)), ((), ())),
                           preferred_element_type=F32)


def _dot_hi(a, b):
    return jnp.dot(a, b, precision=lax.Precision.HIGHEST, preferred_element_type=F32)


def _sigmoid(x):
    return 1.0 / (1.0 + jnp.exp(-x))


def _softplus(x):
    return jnp.maximum(x, 0.0) + jnp.log(1.0 + jnp.exp(-jnp.abs(x)))


def _rms(x, w):
    return x * lax.rsqrt(jnp.mean(x * x, axis=-1, keepdims=True) + NORM_EPS) * w


def _params(sem, vmem=None):
    return pltpu.CompilerParams(dimension_semantics=sem, vmem_limit_bytes=vmem)


def _layout(g):
    gw = g * HEAD_DIM
    order = [("mla_cqkv", MLA_Q_RANK + MLA_KV_RANK),
             ("gdn_q", gw), ("gdn_k", gw), ("gdn_v", gw), ("gdn_gate", gw),
             ("ml_q", gw), ("ml_k", gw), ("ml_v", gw), ("ml_og", gw), ("gqa_q", gw),
             ("gqa_k", gw // 2), ("gqa_v", gw // 2),
             ("gdn_ab", LANE), ("ml_if", LANE), ("mla_kpe", LANE)]
    off, pos = {}, 0
    for name, width in order:
        assert pos % width == 0, (name, pos, width)
        off[name] = pos
        pos += width
    return off, pos


def _packed_columns(g):
    gw = g * HEAD_DIM
    off, total = _layout(g)
    n_pad = -(-total // 512) * 512
    idx = [-1] * n_pad

    def put(name, src0, n, dst0=0):
        for t in range(n):
            idx[off[name] + dst0 + t] = src0 + t

    p = 0
    put("gdn_q", p, gw); p += gw
    put("gdn_k", p, gw); p += gw
    put("gdn_v", p, gw); p += gw
    put("gdn_gate", p, gw); p += gw
    put("gdn_ab", p, 4 * g); p += 4 * g
    put("gqa_q", p, gw); p += gw
    put("gqa_k", p, gw // 2); p += gw // 2
    put("gqa_v", p, gw // 2); p += gw // 2
    for name in ("ml_q", "ml_k"):
        for h in range(g):
            put(name, p + h * ML_DQK, ML_DQK, h * HEAD_DIM)
        p += g * ML_DQK
    put("ml_v", p, gw); p += gw
    put("ml_og", p, gw); p += gw
    put("ml_if", p, 4 * g); p += 4 * g
    put("mla_cqkv", p, MLA_Q_RANK + MLA_KV_RANK); p += MLA_Q_RANK + MLA_KV_RANK
    put("mla_kpe", p, MLA_ROPE); p += MLA_ROPE
    return idx, n_pad, p


def _pick_tile(n, candidates):
    for c in candidates:
        if n % c == 0:
            return c
    raise ValueError(f"no tile for {n} in {candidates}")


def _row_tile(t):
    for c in range(640, 15, -16):
        if t % c == 0:
            return c
    raise ValueError(f"no row tile for {t}")


def _mod_kernel(c_ref, w_ref, b_ref, o_ref):
    c = c_ref[...]
    o_ref[0] = _dot(c * _sigmoid(c), w_ref[0]) + b_ref[0]


def _mod_all(c_rows, w_mod, b_mod):
    depth, d, n = w_mod.shape
    r = c_rows.shape[0]
    tn = _pick_tile(n, (1024, 512, 256, 128))
    return pl.pallas_call(
        _mod_kernel,
        name="mod",
        grid=(depth, n // tn),
        in_specs=[pl.BlockSpec((r, d), lambda l, j: (0, 0)),
                  pl.BlockSpec((1, d, tn), lambda l, j: (l, 0, j)),
                  pl.BlockSpec((1, 1, tn), lambda l, j: (l, 0, j))],
        out_specs=pl.BlockSpec((1, r, tn), lambda l, j: (l, 0, j)),
        out_shape=jax.ShapeDtypeStruct((depth, r, n), F32),
        compiler_params=_params(("parallel", "parallel"), VMEM_LIMIT),
    )(c_rows, w_mod, b_mod.reshape(depth, 1, n))


def _mod_rows(mod_ref, idx, row, ctx_len):
    return jnp.where(row < ctx_len, mod_ref[0, 0, idx:idx + 1, :], mod_ref[0, 1, idx:idx + 1, :])


def _inproj_kernel(h_ref, mod_ref, g_ref, w_ref, o_ref, xn_ref, *, ctx_len, tm):
    i = pl.program_id(1)

    @pl.when(pl.program_id(2) == 0)
    def _():
        row = i * tm + lax.broadcasted_iota(jnp.int32, (tm, 1), 0)
        y = _rms(h_ref[0], g_ref[0:1, :])
        y = y * (1.0 + _mod_rows(mod_ref, 1, row, ctx_len)) + _mod_rows(mod_ref, 0, row, ctx_len)
        xn_ref[...] = y.astype(BF16)

    o_ref[0] = jnp.dot(xn_ref[...], w_ref[...], preferred_element_type=F32)


def _inproj(h, mod, g4, w, ctx_len):
    b, t, d = h.shape
    n = w.shape[1]
    tm = _row_tile(t)
    tn = _pick_tile(n, (1536, 1024, 512))
    return pl.pallas_call(
        functools.partial(_inproj_kernel, ctx_len=ctx_len, tm=tm),
        name="inproj",
        grid=(b, t // tm, n // tn),
        in_specs=[pl.BlockSpec((1, tm, d), lambda bi, i, j: (bi, i, 0)),
                  pl.BlockSpec((1, 2, 6, d), lambda bi, i, j: (bi, 0, 0, 0)),
                  pl.BlockSpec((4, d), lambda bi, i, j: (0, 0)),
                  pl.BlockSpec((d, tn), lambda bi, i, j: (0, j))],
        out_specs=pl.BlockSpec((1, tm, tn), lambda bi, i, j: (bi, i, j)),
        out_shape=jax.ShapeDtypeStruct((b, t, n), F32),
        scratch_shapes=[pltpu.VMEM((tm, d), BF16)],
        compiler_params=_params(("parallel", "parallel", "arbitrary"), VMEM_LIMIT),
    )(h, mod, g4, w)


def _gdn_prep_kernel(u_ref, w_ref, o_ref, *, g, ctx_len):
    j = pl.program_id(1)
    x = u_ref[0]
    t = x.shape[0]
    row = lax.broadcasted_iota(jnp.int32, (t, 1), 0)
    first = (row == 0) | (row == ctx_len)
    last = (row == ctx_len - 1) | (row == t - 1)
    x_prev = jnp.where(first, 0.0, pltpu.roll(x, 1, 0))
    x_next = jnp.where(last, 0.0, pltpu.roll(x, t - 1, 0))
    y = x_prev * w_ref[0:1, :] + x * w_ref[1:2, :] + x_next * w_ref[2:3, :]
    y = y * _sigmoid(y)
    yn = y * lax.rsqrt(jnp.sum(y * y, axis=-1, keepdims=True) + NORM_EPS)
    scale = jnp.where(j < g, HEAD_DIM ** -0.5, 1.0)
    o_ref[0] = jnp.where(j < 2 * g, yn * scale, y)


def _gdn_prep(u, conv_w, g, ctx_len, off):
    b, t, _ = u.shape
    blk0 = off["gdn_q"] // LANE
    return pl.pallas_call(
        functools.partial(_gdn_prep_kernel, g=g, ctx_len=ctx_len),
        name="gdn_prep",
        grid=(b, 3 * g),
        in_specs=[pl.BlockSpec((1, t, LANE), lambda bi, j: (bi, 0, blk0 + j)),
                  pl.BlockSpec((3, LANE), lambda bi, j: (0, j))],
        out_specs=pl.BlockSpec((1, t, LANE), lambda bi, j: (bi, 0, j)),
        out_shape=jax.ShapeDtypeStruct((b, t, 3 * g * LANE), F32),
        compiler_params=_params(("parallel", "parallel"), VMEM_LIMIT),
    )(u, conv_w)


def _chunk_masks(fwd):
    ri = lax.broadcasted_iota(jnp.int32, (CHUNK, CHUNK), 0)
    ci = lax.broadcasted_iota(jnp.int32, (CHUNK, CHUNK), 1)
    if fwd:
        return ri >= ci, ri > ci, ri == ci
    return ri <= ci, ri < ci, ri == ci


def _bcast(col):
    return jnp.broadcast_to(col, (CHUNK, LANE))


def _tri_inverse(a, diag):
    ri = lax.broadcasted_iota(jnp.int32, (CHUNK, CHUNK), 0)
    ci = lax.broadcasted_iota(jnp.int32, (CHUNK, CHUNK), 1)
    t = jnp.where((ri >> 1) == (ci >> 1), diag.astype(F32) - a, 0.0)
    for lb in range(1, int(math.log2(CHUNK))):
        o = jnp.where(((ri >> (lb + 1)) == (ci >> (lb + 1))) & ((ri >> lb) != (ci >> lb)), a, 0.0)
        t = t - _dot_hi(t, _dot_hi(o, t))
    return t


def _gdn_stream(q, k, v, la_col, be_col, s, fwd):
    incl, strict, diag = _chunk_masks(fwd)
    inclf, strictf = incl.astype(F32), strict.astype(F32)
    a_b = _bcast(la_col)
    g_b = _dot_hi(inclf, a_b)
    e = _dot_hi(inclf, a_b[:, :CHUNK] * strictf)
    decay = jnp.where(incl, jnp.exp(e), 0.0)
    beta_b = _bcast(be_col)
    kb = k * beta_b
    a = jnp.where(strict, _dot_nt(kb, k) * decay, 0.0)
    p = _tri_inverse(a, diag)
    eg = jnp.exp(g_b)
    vv = _dot(p, v * beta_b)
    ww = _dot(p, kb * eg)
    attn = jnp.where(incl, _dot_nt(q, k) * decay, 0.0)
    v_new = vv - _dot(ww, s)
    o = _dot(q * eg, s) + _dot(attn, v_new)
    g_end = g_b[CHUNK - 1:CHUNK, :] if fwd else g_b[0:1, :]
    s_new = s * jnp.exp(g_end) + _dot_tn(k * jnp.exp(g_end - g_b), v_new)
    return o, s_new


def _gdn_scan_kernel(qf_ref, gf_ref, qb_ref, gb_ref, al_ref, dt_ref, of_ref, ob_ref, s_ref, *, g):
    @pl.when(pl.program_id(1) == 0)
    def _():
        s_ref[...] = jnp.zeros_like(s_ref)

    gw = g * HEAD_DIM
    for d, (q_ref, gt_ref, o_ref) in enumerate(((qf_ref, gf_ref, of_ref), (qb_ref, gb_ref, ob_ref))):
        gates = gt_ref[0]
        la = -jnp.exp(al_ref[...]) * _softplus(gates + dt_ref[...])
        be = _sigmoid(gates)
        for h in range(g):
            c = d * g + h
            q = q_ref[0, :, h * HEAD_DIM:(h + 1) * HEAD_DIM]
            k = q_ref[0, :, gw + h * HEAD_DIM:gw + (h + 1) * HEAD_DIM]
            v = q_ref[0, :, 2 * gw + h * HEAD_DIM:2 * gw + (h + 1) * HEAD_DIM]
            o, s_new = _gdn_stream(q, k, v, la[:, c:c + 1], be[:, 2 * g + c:2 * g + c + 1],
                                   s_ref[c], d == 0)
            o_ref[0, :, h * HEAD_DIM:(h + 1) * HEAD_DIM] = o
            s_ref[c] = s_new


def _scan_chunk_maps(n_ctx, n_all):
    def fwd(bi, s):
        return (bi, s, 0)

    def bwd(bi, s):
        return (bi, jnp.where(s < n_ctx, n_ctx - 1 - s, n_all + n_ctx - 1 - s), 0)

    return fwd, bwd


def _gdn_scan(qkv, u, a_log_row, dt_row, g, ctx_len, off):
    b, t, _ = qkv.shape
    n_all, n_ctx = t // CHUNK, ctx_len // CHUNK
    fwd, bwd = _scan_chunk_maps(n_ctx, n_all)
    gblk = off["gdn_ab"] // LANE

    def gate_map(m):
        return lambda bi, s: (m(bi, s)[0], m(bi, s)[1], gblk)

    gw = g * HEAD_DIM
    return pl.pallas_call(
        functools.partial(_gdn_scan_kernel, g=g),
        name="gdn_scan",
        grid=(b, n_all),
        in_specs=[pl.BlockSpec((1, CHUNK, 3 * gw), fwd),
                  pl.BlockSpec((1, CHUNK, LANE), gate_map(fwd)),
                  pl.BlockSpec((1, CHUNK, 3 * gw), bwd),
                  pl.BlockSpec((1, CHUNK, LANE), gate_map(bwd)),
                  pl.BlockSpec((1, LANE), lambda bi, s: (0, 0)),
                  pl.BlockSpec((1, LANE), lambda bi, s: (0, 0))],
        out_specs=[pl.BlockSpec((1, CHUNK, gw), fwd), pl.BlockSpec((1, CHUNK, gw), bwd)],
        out_shape=[jax.ShapeDtypeStruct((b, t, gw), F32)] * 2,
        scratch_shapes=[pltpu.VMEM((2 * g, HEAD_DIM, HEAD_DIM), F32)],
        compiler_params=_params(("parallel", "arbitrary")),
    )(qkv, u, qkv, u, a_log_row, dt_row)


def _mlstm_stream(q, k, v, ip_col, fl_col, cm, nrow, mrow, fwd):
    incl, strict, diag = _chunk_masks(fwd)
    inclf, strictf = incl.astype(F32), strict.astype(F32)
    f_b = _bcast(fl_col)
    b_b = _dot_hi(inclf, f_b)
    e = _dot_hi(inclf, f_b[:, :CHUNK] * strictf)
    ip_b = _bcast(ip_col)
    r = _dot_hi(jnp.ones((CHUNK, CHUNK), F32), diag.astype(F32) * ip_b[:, :CHUNK])
    dmat = jnp.where(incl, e + r, -jnp.inf)
    qk = _dot_nt(q, k)
    b_end = b_b[CHUNK - 1:CHUNK, :] if fwd else b_b[0:1, :]
    w_b = b_end - b_b + ip_b
    inter = b_b + mrow
    m_t = jnp.maximum(inter, jnp.max(dmat, axis=1, keepdims=True))
    s_inter = jnp.exp(inter - m_t)
    p = jnp.exp(dmat - m_t[:, :CHUNK]) * qk
    num = s_inter * _dot(q, cm) + _dot(p, v)
    den = s_inter * jnp.sum(q * nrow, axis=1, keepdims=True) + jnp.sum(p, axis=1, keepdims=True)
    h = num / jnp.maximum(jnp.abs(den), jnp.exp(-m_t))
    m_new = jnp.maximum(b_end + mrow, jnp.max(w_b, axis=0, keepdims=True))
    s_old = jnp.exp(b_end + mrow - m_new)
    wk = jnp.exp(w_b - m_new) * k
    cm_new = s_old * cm + _dot_tn(wk, v)
    n_new = s_old * nrow + jnp.sum(wk, axis=0, keepdims=True)
    return h, cm_new, n_new, m_new


def _mlstm_scan_kernel(qf_ref, kf_ref, vf_ref, gf_ref, qb_ref, kb_ref, vb_ref, gb_ref, bias_ref,
                       of_ref, ob_ref, c_ref, n_ref, m_ref, *, g):
    @pl.when(pl.program_id(1) == 0)
    def _():
        c_ref[...] = jnp.zeros_like(c_ref)
        n_ref[...] = jnp.zeros_like(n_ref)
        m_ref[...] = jnp.zeros_like(m_ref)

    dirs = ((qf_ref, kf_ref, vf_ref, gf_ref, of_ref), (qb_ref, kb_ref, vb_ref, gb_ref, ob_ref))
    for d, (q_ref, k_ref, v_ref, gt_ref, o_ref) in enumerate(dirs):
        gates = gt_ref[0] + bias_ref[...]
        logsig = jnp.minimum(gates, 0.0) - jnp.log(1.0 + jnp.exp(-jnp.abs(gates)))
        for h in range(g):
            c = d * g + h
            sl = slice(h * HEAD_DIM, (h + 1) * HEAD_DIM)
            q = q_ref[0, :, sl] * (ML_DQK ** -0.5)
            hh, cm, nr, mr = _mlstm_stream(q, k_ref[0, :, sl], v_ref[0, :, sl], gates[:, c:c + 1],
                                           logsig[:, 2 * g + c:2 * g + c + 1],
                                           c_ref[c], n_ref[c], m_ref[c], d == 0)
            o_ref[0, :, sl] = hh
            c_ref[c] = cm
            n_ref[c] = nr
            m_ref[c] = mr


def _mlstm_scan(u, bias_row, g, ctx_len, off):
    b, t, _ = u.shape
    n_all, n_ctx = t // CHUNK, ctx_len // CHUNK
    fwd, bwd = _scan_chunk_maps(n_ctx, n_all)
    gw = g * HEAD_DIM

    def col_map(m, blk):
        return lambda bi, s: (m(bi, s)[0], m(bi, s)[1], blk)

    def specs(m):
        return [pl.BlockSpec((1, CHUNK, gw), col_map(m, off["ml_q"] // gw)),
                pl.BlockSpec((1, CHUNK, gw), col_map(m, off["ml_k"] // gw)),
                pl.BlockSpec((1, CHUNK, gw), col_map(m, off["ml_v"] // gw)),
                pl.BlockSpec((1, CHUNK, LANE), col_map(m, off["ml_if"] // LANE))]

    return pl.pallas_call(
        functools.partial(_mlstm_scan_kernel, g=g),
        name="mlstm_scan",
        grid=(b, n_all),
        in_specs=specs(fwd) + specs(bwd) + [pl.BlockSpec((1, LANE), lambda bi, s: (0, 0))],
        out_specs=[pl.BlockSpec((1, CHUNK, gw), fwd), pl.BlockSpec((1, CHUNK, gw), bwd)],
        out_shape=[jax.ShapeDtypeStruct((b, t, gw), F32)] * 2,
        scratch_shapes=[pltpu.VMEM((2 * g, HEAD_DIM, HEAD_DIM), F32),
                        pltpu.VMEM((2 * g, 1, HEAD_DIM), F32),
                        pltpu.VMEM((2 * g, 1, HEAD_DIM), F32)],
        compiler_params=_params(("parallel", "arbitrary")),
    )(u, u, u, u, u, u, u, u, bias_row)


def _rope(x, c, s_up, s_dn, shift):
    return x * c + pltpu.roll(x, LANE - shift, 1) * s_up + pltpu.roll(x, shift, 1) * s_dn


def _gqa_prep_kernel(q_ref, k_ref, v_ref, c_ref, su_ref, sd_ref, n_ref, qo_ref, ko_ref, vo_ref, *, g):
    c, su, sd = c_ref[...], su_ref[...], sd_ref[...]
    scale = HEAD_DIM ** -0.5
    for h in range(g):
        sl = slice(h * HEAD_DIM, (h + 1) * HEAD_DIM)
        qo_ref[0, :, sl] = (_rope(_rms(q_ref[0, :, sl], n_ref[0:1, :]), c, su, sd, 32) * scale).astype(BF16)
    for h in range(g // 2):
        sl = slice(h * HEAD_DIM, (h + 1) * HEAD_DIM)
        ko_ref[0, :, sl] = _rope(_rms(k_ref[0, :, sl], n_ref[1:2, :]), c, su, sd, 32).astype(BF16)
    vo_ref[0] = v_ref[0].astype(BF16)


def _gqa_prep(u, tables, qk_norm, g, off):
    b, t, _ = u.shape
    tm = _row_tile(t)
    gw, kw = g * HEAD_DIM, g * HEAD_DIM // 2
    tab = pl.BlockSpec((tm, LANE), lambda bi, i: (i, 0))
    return pl.pallas_call(
        functools.partial(_gqa_prep_kernel, g=g),
        name="gqa_prep",
        grid=(b, t // tm),
        in_specs=[pl.BlockSpec((1, tm, gw), lambda bi, i: (bi, i, off["gqa_q"] // gw)),
                  pl.BlockSpec((1, tm, kw), lambda bi, i: (bi, i, off["gqa_k"] // kw)),
                  pl.BlockSpec((1, tm, kw), lambda bi, i: (bi, i, off["gqa_v"] // kw)),
                  tab, tab, tab,
                  pl.BlockSpec((2, HEAD_DIM), lambda bi, i: (0, 0))],
        out_specs=[pl.BlockSpec((1, tm, gw), lambda bi, i: (bi, i, 0)),
                   pl.BlockSpec((1, tm, kw), lambda bi, i: (bi, i, 0)),
                   pl.BlockSpec((1, tm, kw), lambda bi, i: (bi, i, 0))],
        out_shape=[jax.ShapeDtypeStruct((b, t, gw), BF16),
                   jax.ShapeDtypeStruct((b, t, kw), BF16),
                   jax.ShapeDtypeStruct((b, t, kw), BF16)],
        compiler_params=_params(("parallel", "parallel")),
    )(u, u, u, *tables, qk_norm)


def _mla_prep_kernel(x_ref, kpe_ref, c_ref, su_ref, sd_ref, qn_ref, kvn_ref, wq_ref, wk_ref, wv_ref,
                     qo_ref, ko_ref, vo_ref, *, g):
    c, su, sd = c_ref[...], su_ref[...], sd_ref[...]
    scale = (MLA_NOPE + MLA_ROPE) ** -0.5
    x = x_ref[0]
    cq = _rms(x[:, :MLA_Q_RANK], qn_ref[...])
    ckv = _rms(x[:, MLA_Q_RANK:], kvn_ref[...])
    q = _dot(cq, wq_ref[...])
    kn = _dot(ckv, wk_ref[...])
    vo_ref[0] = _dot(ckv, wv_ref[...]).astype(BF16)
    kpe = _rope(kpe_ref[0], c, su, sd, 16).astype(BF16)
    for h in range(g):
        lo = h * MLA_QK_PAD
        qo_ref[0, :, lo:lo + LANE] = (q[:, lo:lo + LANE] * scale).astype(BF16)
        qo_ref[0, :, lo + LANE:lo + 2 * LANE] = (
            _rope(q[:, lo + LANE:lo + 2 * LANE], c, su, sd, 16) * scale).astype(BF16)
        ko_ref[0, :, lo:lo + LANE] = kn[:, h * LANE:(h + 1) * LANE].astype(BF16)
        ko_ref[0, :, lo + LANE:lo + 2 * LANE] = kpe


def _mla_prep(u, tables, q_norm, kv_norm, wq, wk, wv, g, off):
    b, t, _ = u.shape
    tm = _row_tile(t)
    cw = MLA_Q_RANK + MLA_KV_RANK
    tab = pl.BlockSpec((tm, LANE), lambda bi, i: (i, 0))

    def full(a):
        return pl.BlockSpec(a.shape, lambda bi, i: (0,) * a.ndim)

    return pl.pallas_call(
        functools.partial(_mla_prep_kernel, g=g),
        name="mla_prep",
        grid=(b, t // tm),
        in_specs=[pl.BlockSpec((1, tm, cw), lambda bi, i: (bi, i, off["mla_cqkv"] // cw)),
                  pl.BlockSpec((1, tm, LANE), lambda bi, i: (bi, i, off["mla_kpe"] // LANE)),
                  tab, tab, tab, full(q_norm), full(kv_norm), full(wq), full(wk), full(wv)],
        out_specs=[pl.BlockSpec((1, tm, g * MLA_QK_PAD), lambda bi, i: (bi, i, 0)),
                   pl.BlockSpec((1, tm, g * MLA_QK_PAD), lambda bi, i: (bi, i, 0)),
                   pl.BlockSpec((1, tm, g * MLA_DV), lambda bi, i: (bi, i, 0))],
        out_shape=[jax.ShapeDtypeStruct((b, t, g * MLA_QK_PAD), BF16),
                   jax.ShapeDtypeStruct((b, t, g * MLA_QK_PAD), BF16),
                   jax.ShapeDtypeStruct((b, t, g * MLA_DV), BF16)],
        compiler_params=_params(("parallel", "parallel")),
    )(u, u, *tables, q_norm, kv_norm, wq, wk, wv)


def _attn_kernel(q_ref, k_ref, v_ref, o_ref, *, ctx_len, tq):
    q = q_ref[0]

    def attend(nk):
        s = lax.dot_general(q, k_ref[0, :nk, :], (((1,), (1,)), ((), ())), preferred_element_type=F32)
        p = jnp.exp(s - jnp.max(s, axis=-1, keepdims=True))
        l = jnp.sum(p, axis=-1, keepdims=True)
        o = jnp.dot(p.astype(BF16), v_ref[0, :nk, :], preferred_element_type=F32)
        o_ref[0] = (o / l).astype(o_ref.dtype)

    is_ctx = pl.program_id(2) * tq < ctx_len

    @pl.when(is_ctx)
    def _():
        attend(ctx_len)

    @pl.when(jnp.logical_not(is_ctx))
    def _():
        attend(k_ref.shape[1])


def _attention(q, k, v, heads, kv_heads, ctx_len):
    b, t, _ = q.shape
    dq, dv = q.shape[2] // heads, v.shape[2] // kv_heads
    rep = heads // kv_heads
    tq = math.gcd(256, ctx_len)
    return pl.pallas_call(
        functools.partial(_attn_kernel, ctx_len=ctx_len, tq=tq),
        name="attn",
        grid=(b, heads, t // tq),
        in_specs=[pl.BlockSpec((1, tq, dq), lambda bi, h, i: (bi, i, h)),
                  pl.BlockSpec((1, t, dq), lambda bi, h, i: (bi, 0, h // rep)),
                  pl.BlockSpec((1, t, dv), lambda bi, h, i: (bi, 0, h // rep))],
        out_specs=pl.BlockSpec((1, tq, dv), lambda bi, h, i: (bi, i, h)),
        out_shape=jax.ShapeDtypeStruct((b, t, heads * dv), BF16),
        compiler_params=_params(("parallel", "parallel", "parallel"), VMEM_LIMIT),
    )(q, k, v)


def _head_norm(x, w_row, g):
    parts = []
    for h in range(g):
        xs = x[:, h * HEAD_DIM:(h + 1) * HEAD_DIM]
        parts.append(xs * lax.rsqrt(jnp.mean(xs * xs, axis=-1, keepdims=True) + NORM_EPS))
    return jnp.concatenate(parts, axis=-1) * w_row


def _outproj_kernel(af_ref, ab_ref, ga_ref, yb_ref, cf_ref, cb_ref, gc_ref, yd_ref, h_ref, mod_ref, g_ref,
                    na_ref, nc_ref, w_ref, o_ref, *, g, ctx_len, tm):
    row = pl.program_id(1) * tm + lax.broadcasted_iota(jnp.int32, (tm, 1), 0)
    ga = ga_ref[0]
    ya = _head_norm(af_ref[0] + ab_ref[0], na_ref[...], g) * (ga * _sigmoid(ga))
    yc = _head_norm(cf_ref[0] + cb_ref[0], nc_ref[...], g) * _sigmoid(gc_ref[0])
    y = jnp.concatenate([ya.astype(BF16), yb_ref[0], yc.astype(BF16), yd_ref[0]], axis=-1)
    z = jnp.dot(y, w_ref[...], preferred_element_type=F32)
    o_ref[0] = h_ref[0] + _mod_rows(mod_ref, 2, row, ctx_len) * _rms(z, g_ref[1:2, :])


def _outproj(oa, ob, oc, od, u, h, mod, g4, na_row, nc_row, w, g, ctx_len, off):
    b, t, d = h.shape
    tm = _row_tile(t)
    gw = g * HEAD_DIM

    def rows(width, blk=0):
        return pl.BlockSpec((1, tm, width), lambda bi, i: (bi, i, blk))

    return pl.pallas_call(
        functools.partial(_outproj_kernel, g=g, ctx_len=ctx_len, tm=tm),
        name="outproj",
        grid=(b, t // tm),
        in_specs=[rows(gw), rows(gw), rows(gw, off["gdn_gate"] // gw), rows(gw),
                  rows(gw), rows(gw), rows(gw, off["ml_og"] // gw), rows(gw),
                  rows(d),
                  pl.BlockSpec((1, 2, 6, d), lambda bi, i: (bi, 0, 0, 0)),
                  pl.BlockSpec((4, d), lambda bi, i: (0, 0)),
                  pl.BlockSpec((1, gw), lambda bi, i: (0, 0)),
                  pl.BlockSpec((1, gw), lambda bi, i: (0, 0)),
                  pl.BlockSpec(w.shape, lambda bi, i: (0, 0))],
        out_specs=rows(d),
        out_shape=jax.ShapeDtypeStruct((b, t, d), F32),
        compiler_params=_params(("parallel", "parallel"), VMEM_LIMIT),
    )(oa[0], oa[1], u, ob, oc[0], oc[1], u, od, h, mod, g4, na_row, nc_row, w)


def _ffn_kernel(h_ref, mod_ref, g_ref, w1_ref, w2_ref, o_ref, xn_ref, acc_ref, *, ctx_len, tm):
    k = pl.program_id(2)
    row = pl.program_id(1) * tm + lax.broadcasted_iota(jnp.int32, (tm, 1), 0)

    @pl.when(k == 0)
    def _():
        y = _rms(h_ref[0], g_ref[2:3, :])
        y = y * (1.0 + _mod_rows(mod_ref, 4, row, ctx_len)) + _mod_rows(mod_ref, 3, row, ctx_len)
        xn_ref[...] = y.astype(BF16)
        acc_ref[...] = jnp.zeros_like(acc_ref)

    hid = jnp.maximum(jnp.dot(xn_ref[...], w1_ref[...], preferred_element_type=F32), 0.0)
    acc_ref[...] += jnp.dot((hid * hid).astype(BF16), w2_ref[...], preferred_element_type=F32)

    @pl.when(k == pl.num_programs(2) - 1)
    def _():
        o_ref[0] = h_ref[0] + _mod_rows(mod_ref, 5, row, ctx_len) * _rms(acc_ref[...], g_ref[3:4, :])


def _ffn(h, mod, g4, w1, w2, ctx_len):
    b, t, d = h.shape
    ff = w1.shape[1]
    tm = _row_tile(t)
    tf = _pick_tile(ff, (512, 256, 128))
    return pl.pallas_call(
        functools.partial(_ffn_kernel, ctx_len=ctx_len, tm=tm),
        name="ffn",
        grid=(b, t // tm, ff // tf),
        in_specs=[pl.BlockSpec((1, tm, d), lambda bi, i, k: (bi, i, 0)),
                  pl.BlockSpec((1, 2, 6, d), lambda bi, i, k: (bi, 0, 0, 0)),
                  pl.BlockSpec((4, d), lambda bi, i, k: (0, 0)),
                  pl.BlockSpec((d, tf), lambda bi, i, k: (0, k)),
                  pl.BlockSpec((tf, d), lambda bi, i, k: (k, 0))],
        out_specs=pl.BlockSpec((1, tm, d), lambda bi, i, k: (bi, i, 0)),
        out_shape=jax.ShapeDtypeStruct((b, t, d), F32),
        scratch_shapes=[pltpu.VMEM((tm, d), BF16), pltpu.VMEM((tm, d), F32)],
        compiler_params=_params(("parallel", "parallel", "arbitrary"), VMEM_LIMIT),
    )(h, mod, g4, w1, w2)


def _rope_tables(seq, ctx_len, dim):
    quarter = dim // 4
    t = jnp.arange(seq, dtype=jnp.int32)
    inv_freq = ROPE_BASE ** (-jnp.arange(quarter, dtype=F32) / quarter)
    ang_r = (t // GRID_W).astype(F32)[:, None] * inv_freq
    ang_c = (t % GRID_W).astype(F32)[:, None] * inv_freq
    zero = jnp.zeros_like(ang_r)
    cos = jnp.concatenate([jnp.cos(ang_r)] * 2 + [jnp.cos(ang_c)] * 2, axis=-1)
    s_up = jnp.concatenate([-jnp.sin(ang_r), zero, -jnp.sin(ang_c), zero], axis=-1)
    s_dn = jnp.concatenate([zero, jnp.sin(ang_r), zero, jnp.sin(ang_c)], axis=-1)

    def finish(tab, fill):
        tab = jnp.pad(tab, ((0, 0), (0, LANE - dim)), constant_values=fill)
        return jnp.pad(tab, ((ctx_len, 0), (0, 0)), constant_values=fill)

    return finish(cos, 1.0), finish(s_up, 0.0), finish(s_dn, 0.0)


def _lane_row(v):
    v = v.reshape(1, -1)
    return jnp.pad(v, ((0, 0), (0, LANE - v.shape[1])))


def kernel(x, c, ctx, c_ctx, w_mod, b_mod, g_norm, w_in, gdn_conv, gdn_a_log, gdn_dt_bias, gdn_norm,
           gqa_qk_norm, mlstm_gate_bias, mlstm_norm, mla_q_norm, mla_kv_norm, mla_w_qb, mla_w_kvb,
           w_out, w_ff1, w_ff2):
    bsz, seq, d = x.shape
    ctx_len = ctx.shape[1]
    depth = w_in.shape[0]
    g = w_out.shape[1] // (4 * HEAD_DIM)
    assert ctx_len % CHUNK == 0 and seq % CHUNK == 0 and seq % GRID_W == 0
    off, _ = _layout(g)
    cols, n_pad, n_src = _packed_columns(g)
    assert n_src == w_in.shape[2]

    cols = jnp.asarray(cols, jnp.int32)
    w_in_p = jnp.where(cols >= 0, jnp.take(w_in, jnp.maximum(cols, 0), axis=2), 0.0).astype(BF16)
    wq = mla_w_qb.reshape(depth, MLA_Q_RANK, g, MLA_NOPE + MLA_ROPE)
    wq = jnp.pad(wq, ((0, 0), (0, 0), (0, 0), (0, MLA_QK_PAD - MLA_NOPE - MLA_ROPE)))
    wq = wq.reshape(depth, MLA_Q_RANK, g * MLA_QK_PAD).astype(BF16)
    wkv = mla_w_kvb.reshape(depth, MLA_KV_RANK, g, MLA_NOPE + MLA_DV)
    wk = wkv[..., :MLA_NOPE].reshape(depth, MLA_KV_RANK, g * MLA_NOPE).astype(BF16)
    wv = wkv[..., MLA_NOPE:].reshape(depth, MLA_KV_RANK, g * MLA_DV).astype(BF16)
    w_out_b, w_ff1_b, w_ff2_b = w_out.astype(BF16), w_ff1.astype(BF16), w_ff2.astype(BF16)
    gqa_tab = _rope_tables(seq, ctx_len, HEAD_DIM)
    mla_tab = _rope_tables(seq, ctx_len, MLA_ROPE)

    n_rows = -(-(bsz + 1) // 8) * 8
    c_rows = jnp.concatenate([c, c_ctx[None, :], jnp.zeros((n_rows - bsz - 1, d), F32)], axis=0)
    mod_all = _mod_all(c_rows, w_mod, b_mod).reshape(depth, n_rows, 6, d)
    mod_all = jnp.stack([jnp.broadcast_to(mod_all[:, bsz:bsz + 1], (depth, bsz, 6, d)), mod_all[:, :bsz]], axis=2)

    h = jnp.concatenate([ctx, x], axis=1)
    for l in range(depth):
        mod, g4 = mod_all[l], g_norm[l]
        u = _inproj(h, mod, g4, w_in_p[l], ctx_len)
        qkv = _gdn_prep(u, gdn_conv[l], g, ctx_len, off)
        oa = _gdn_scan(qkv, u, _lane_row(gdn_a_log[l]), _lane_row(gdn_dt_bias[l]), g, ctx_len, off)
        qn, kn, vn = _gqa_prep(u, gqa_tab, gqa_qk_norm[l], g, off)
        ob = _attention(qn, kn, vn, g, g // 2, ctx_len)
        oc = _mlstm_scan(u, _lane_row(mlstm_gate_bias[l]), g, ctx_len, off)
        qm, km, vm = _mla_prep(u, mla_tab, mla_q_norm[l].reshape(1, -1), mla_kv_norm[l].reshape(1, -1),
                               wq[l], wk[l], wv[l], g, off)
        od = _attention(qm, km, vm, g, g, ctx_len)
        h = _outproj(oa, ob, oc, od, u, h, mod, g4, jnp.tile(gdn_norm[l], g).reshape(1, -1),
                     mlstm_norm[l].reshape(1, -1), w_out_b[l], g, ctx_len, off)
        h = _ffn(h, mod, g4, w_ff1_b[l], w_ff2_b[l], ctx_len)
    return h[:, ctx_len:, :]
```

```python
import functools
import math

import jax
import jax.numpy as jnp
from jax import lax
from jax.experimental import pallas as pl
from jax.experimental.pallas import tpu as pltpu

F32 = jnp.float32
BF16 = jnp.bfloat16

LANE = 128
HEAD_DIM = 128
CHUNK = 64
GRID_W = 64
ROPE_BASE = 10000.0
NORM_EPS = 1e-6
ML_DQK = 64
MLA_Q_RANK = 384
MLA_KV_RANK = 128
MLA_NOPE = 128
MLA_ROPE = 64
MLA_DV = 128
MLA_QK_PAD = 256
VMEM_LIMIT = 56 * 1024 * 1024


def _dot(a, b):
    return jnp.dot(a.astype(BF16), b.astype(BF16), preferred_element_type=F32)


def _dot_nt(a, b):
    return lax.dot_general(a.astype(BF16), b.astype(BF16), (((1,), (1,)), ((), ())),
                           preferred_element_type=F32)


def _dot_tn(a, b):
    return lax.dot_general(a.astype(BF16), b.astype(BF16), (((0,), (0,)), ((), ())),
                           preferred_element_type=F32)


def _sigmoid(x):
    return 1.0 / (1.0 + jnp.exp(-x))


def _softplus(x):
    return jnp.maximum(x, 0.0) + jnp.log(1.0 + jnp.exp(-jnp.abs(x)))


def _rms(x, w):
    return x * lax.rsqrt(jnp.mean(x * x, axis=-1, keepdims=True) + NORM_EPS) * w


def _params(sem, vmem=None):
    return pltpu.CompilerParams(dimension_semantics=sem, vmem_limit_bytes=vmem)


def _layout(g):
    gw = g * HEAD_DIM
    order = [("mla_cqkv", MLA_Q_RANK + MLA_KV_RANK),
             ("gdn_q", gw), ("gdn_k", gw), ("gdn_v", gw), ("gdn_gate", gw),
             ("ml_q", gw), ("ml_k", gw), ("ml_v", gw), ("ml_og", gw), ("gqa_q", gw),
             ("gqa_k", gw // 2), ("gqa_v", gw // 2),
             ("gdn_ab", LANE), ("ml_if", LANE), ("mla_kpe", LANE)]
    off, pos = {}, 0
    for name, width in order:
        assert pos % width == 0, (name, pos, width)
        off[name] = pos
        pos += width
    return off, pos


def _packed_columns(g):
    gw = g * HEAD_DIM
    off, total = _layout(g)
    n_pad = -(-total // 512) * 512
    idx = [-1] * n_pad

    def put(name, src0, n, dst0=0):
        for t in range(n):
            idx[off[name] + dst0 + t] = src0 + t

    p = 0
    put("gdn_q", p, gw); p += gw
    put("gdn_k", p, gw); p += gw
    put("gdn_v", p, gw); p += gw
    put("gdn_gate", p, gw); p += gw
    put("gdn_ab", p, 4 * g); p += 4 * g
    put("gqa_q", p, gw); p += gw
    put("gqa_k", p, gw // 2); p += gw // 2
    put("gqa_v", p, gw // 2); p += gw // 2
    for name in ("ml_q", "ml_k"):
        for h in range(g):
            put(name, p + h * ML_DQK, ML_DQK, h * HEAD_DIM)
        p += g * ML_DQK
    put("ml_v", p, gw); p += gw
    put("ml_og", p, gw); p += gw
    put("ml_if", p, 4 * g); p += 4 * g
    put("mla_cqkv", p, MLA_Q_RANK + MLA_KV_RANK); p += MLA_Q_RANK + MLA_KV_RANK
    put("mla_kpe", p, MLA_ROPE); p += MLA_ROPE
    return idx, n_pad, p


def _pick_tile(n, candidates):
    for c in candidates:
        if n % c == 0:
            return c
    raise ValueError(f"no tile for {n} in {candidates}")


def _row_tile(t):
    for c in range(640, 15, -16):
        if t % c == 0:
            return c
    raise ValueError(f"no row tile for {t}")


def _mod_kernel(c_ref, w_ref, b_ref, o_ref):
    c = c_ref[...]
    o_ref[0] = _dot(c * _sigmoid(c), w_ref[0]) + b_ref[0]


def _mod_all(c_rows, w_mod, b_mod):
    depth, d, n = w_mod.shape
    r = c_rows.shape[0]
    tn = _pick_tile(n, (1024, 512, 256, 128))
    return pl.pallas_call(
        _mod_kernel,
        name="mod",
        grid=(depth, n // tn),
        in_specs=[pl.BlockSpec((r, d), lambda l, j: (0, 0)),
                  pl.BlockSpec((1, d, tn), lambda l, j: (l, 0, j)),
                  pl.BlockSpec((1, 1, tn), lambda l, j: (l, 0, j))],
        out_specs=pl.BlockSpec((1, r, tn), lambda l, j: (l, 0, j)),
        out_shape=jax.ShapeDtypeStruct((depth, r, n), F32),
        compiler_params=_params(("parallel", "parallel"), VMEM_LIMIT),
    )(c_rows, w_mod, b_mod.reshape(depth, 1, n))


def _mod_rows(mod_ref, idx, row, ctx_len):
    return jnp.where(row < ctx_len, mod_ref[0, 0, idx:idx + 1, :], mod_ref[0, 1, idx:idx + 1, :])


def _inproj_kernel(h_ref, mod_ref, g_ref, w_ref, o_ref, xn_ref, *, ctx_len, tm):
    i = pl.program_id(1)

    @pl.when(pl.program_id(2) == 0)
    def _():
        row = i * tm + lax.broadcasted_iota(jnp.int32, (tm, 1), 0)
        y = _rms(h_ref[0], g_ref[0:1, :])
        y = y * (1.0 + _mod_rows(mod_ref, 1, row, ctx_len)) + _mod_rows(mod_ref, 0, row, ctx_len)
        xn_ref[...] = y.astype(BF16)

    o_ref[0] = jnp.dot(xn_ref[...], w_ref[...], preferred_element_type=F32)


def _inproj(h, mod, g4, w, ctx_len):
    b, t, d = h.shape
    n = w.shape[1]
    tm = _row_tile(t)
    tn = _pick_tile(n, (1536, 1024, 512))
    return pl.pallas_call(
        functools.partial(_inproj_kernel, ctx_len=ctx_len, tm=tm),
        name="inproj",
        grid=(b, t // tm, n // tn),
        in_specs=[pl.BlockSpec((1, tm, d), lambda bi, i, j: (bi, i, 0)),
                  pl.BlockSpec((1, 2, 6, d), lambda bi, i, j: (bi, 0, 0, 0)),
                  pl.BlockSpec((4, d), lambda bi, i, j: (0, 0)),
                  pl.BlockSpec((d, tn), lambda bi, i, j: (0, j))],
        out_specs=pl.BlockSpec((1, tm, tn), lambda bi, i, j: (bi, i, j)),
        out_shape=jax.ShapeDtypeStruct((b, t, n), F32),
        scratch_shapes=[pltpu.VMEM((tm, d), BF16)],
        compiler_params=_params(("parallel", "parallel", "arbitrary"), VMEM_LIMIT),
    )(h, mod, g4, w)


def _gdn_prep_kernel(u_ref, w_ref, o_ref, *, g, ctx_len):
    j = pl.program_id(1)
    x = u_ref[0]
    t = x.shape[0]
    row = lax.broadcasted_iota(jnp.int32, (t, 1), 0)
    first = (row == 0) | (row == ctx_len)
    last = (row == ctx_len - 1) | (row == t - 1)
    x_prev = jnp.where(first, 0.0, pltpu.roll(x, 1, 0))
    x_next = jnp.where(last, 0.0, pltpu.roll(x, t - 1, 0))
    y = x_prev * w_ref[0:1, :] + x * w_ref[1:2, :] + x_next * w_ref[2:3, :]
    y = y * _sigmoid(y)
    yn = y * lax.rsqrt(jnp.sum(y * y, axis=-1, keepdims=True) + NORM_EPS)
    scale = jnp.where(j < g, HEAD_DIM ** -0.5, 1.0)
    o_ref[0] = jnp.where(j < 2 * g, yn * scale, y)


def _gdn_prep(u, conv_w, g, ctx_len, off):
    b, t, _ = u.shape
    blk0 = off["gdn_q"] // LANE
    return pl.pallas_call(
        functools.partial(_gdn_prep_kernel, g=g, ctx_len=ctx_len),
        name="gdn_prep",
        grid=(b, 3 * g),
        in_specs=[pl.BlockSpec((1, t, LANE), lambda bi, j: (bi, 0, blk0 + j)),
                  pl.BlockSpec((3, LANE), lambda bi, j: (0, j))],
        out_specs=pl.BlockSpec((1, t, LANE), lambda bi, j: (bi, 0, j)),
        out_shape=jax.ShapeDtypeStruct((b, t, 3 * g * LANE), F32),
        compiler_params=_params(("parallel", "parallel"), VMEM_LIMIT),
    )(u, conv_w)


def _chunk_masks(fwd):
    ri = lax.broadcasted_iota(jnp.int32, (CHUNK, CHUNK), 0)
    ci = lax.broadcasted_iota(jnp.int32, (CHUNK, CHUNK), 1)
    if fwd:
        return ri >= ci, ri > ci, ri == ci
    return ri <= ci, ri < ci, ri == ci


def _bcast(col):
    return jnp.broadcast_to(col, (CHUNK, LANE))


def _split3(x):
    x1 = x.astype(BF16)
    r = x - x1.astype(F32)
    x2 = r.astype(BF16)
    return x1, x2, (r - x2.astype(F32)).astype(BF16)


def _chunk_cumsum(x, incl):
    lmat = jnp.where(incl, 1.0, 0.0).astype(BF16)
    p1, p2, p3 = _split3(x)
    return (jnp.dot(lmat, p1, preferred_element_type=F32) + jnp.dot(lmat, p2, preferred_element_type=F32)
            + jnp.dot(lmat, p3, preferred_element_type=F32))


def _pair_diff(g_b):
    return g_b[:, :CHUNK] - g_b.T[:CHUNK, :]


def _tri_inverse_all(a_list, diag):
    ri = lax.broadcasted_iota(jnp.int32, (CHUNK, CHUNK), 0)
    ci = lax.broadcasted_iota(jnp.int32, (CHUNK, CHUNK), 1)
    eye = diag.astype(F32)
    ts = [jnp.where((ri >> 1) == (ci >> 1), eye - a, 0.0) for a in a_list]
    for lb in range(1, int(math.log2(CHUNK))):
        join = ((ri >> (lb + 1)) == (ci >> (lb + 1))) & ((ri >> lb) != (ci >> lb))
        us = [_dot(jnp.where(join, a, 0.0), t) for a, t in zip(a_list, ts)]
        ts = [t - _dot(t, u) for t, u in zip(ts, us)]
    return ts


def _gdn_scan_kernel(qf_ref, gf_ref, qb_ref, gb_ref, al_ref, dt_ref, of_ref, ob_ref, s_ref, *, g):
    @pl.when(pl.program_id(1) == 0)
    def _():
        s_ref[...] = jnp.zeros_like(s_ref)

    gw = g * HEAD_DIM
    streams = range(2 * g)
    fwd, incl, strict, q, k, v, g_b, beta_b = [], [], [], [], [], [], [], []
    diag = _chunk_masks(True)[2]
    for d, (q_ref, gt_ref) in enumerate(((qf_ref, gf_ref), (qb_ref, gb_ref))):
        m_incl, m_strict, _ = _chunk_masks(d == 0)
        gates = gt_ref[0]
        la = -jnp.exp(al_ref[...]) * _softplus(gates + dt_ref[...])
        g_all = _chunk_cumsum(la, m_incl)
        be = _sigmoid(gates)
        for h in range(g):
            c = d * g + h
            fwd.append(d == 0)
            incl.append(m_incl)
            strict.append(m_strict)
            q.append(q_ref[0, :, h * HEAD_DIM:(h + 1) * HEAD_DIM])
            k.append(q_ref[0, :, gw + h * HEAD_DIM:gw + (h + 1) * HEAD_DIM])
            v.append(q_ref[0, :, 2 * gw + h * HEAD_DIM:2 * gw + (h + 1) * HEAD_DIM])
            g_b.append(_bcast(g_all[:, c:c + 1]))
            beta_b.append(_bcast(be[:, 2 * g + c:2 * g + c + 1]))

    decay = [jnp.exp(jnp.where(incl[i], _pair_diff(g_b[i]), -jnp.inf)) for i in streams]
    kb = [k[i] * beta_b[i] for i in streams]
    kq = [_dot_nt(jnp.concatenate([kb[i], q[i]], axis=0), k[i]) for i in streams]
    a = [jnp.where(strict[i], kq[i][:CHUNK] * decay[i], 0.0) for i in streams]
    attn = [kq[i][CHUNK:] * decay[i] for i in streams]
    t = _tri_inverse_all(a, diag)
    eg = [jnp.exp(g_b[i]) for i in streams]
    vw = [_dot(t[i], jnp.concatenate([v[i] * beta_b[i], kb[i] * eg[i]], axis=1)) for i in streams]
    s = [s_ref[i] for i in streams]
    ws = [_dot(jnp.concatenate([vw[i][:, HEAD_DIM:], q[i] * eg[i]], axis=0), s[i]) for i in streams]
    v_new = [vw[i][:, :HEAD_DIM] - ws[i][:CHUNK] for i in streams]
    o = [ws[i][CHUNK:] + _dot(attn[i], v_new[i]) for i in streams]
    g_end = [g_b[i][CHUNK - 1:CHUNK, :] if fwd[i] else g_b[i][0:1, :] for i in streams]
    s_new = [s[i] * jnp.exp(g_end[i]) + _dot_tn(k[i] * jnp.exp(g_end[i] - g_b[i]), v_new[i]) for i in streams]
    for i in streams:
        o_ref = of_ref if fwd[i] else ob_ref
        h = i % g
        o_ref[0, :, h * HEAD_DIM:(h + 1) * HEAD_DIM] = o[i]
        s_ref[i] = s_new[i]


def _scan_chunk_maps(n_ctx, n_all):
    def fwd(bi, s):
        return (bi, s, 0)

    def bwd(bi, s):
        return (bi, jnp.where(s < n_ctx, n_ctx - 1 - s, n_all + n_ctx - 1 - s), 0)

    return fwd, bwd


def _gdn_scan(qkv, u, a_log_row, dt_row, g, ctx_len, off):
    b, t, _ = qkv.shape
    n_all, n_ctx = t // CHUNK, ctx_len // CHUNK
    fwd, bwd = _scan_chunk_maps(n_ctx, n_all)
    gblk = off["gdn_ab"] // LANE

    def gate_map(m):
        return lambda bi, s: (m(bi, s)[0], m(bi, s)[1], gblk)

    gw = g * HEAD_DIM
    return pl.pallas_call(
        functools.partial(_gdn_scan_kernel, g=g),
        name="gdn_scan",
        grid=(b, n_all),
        in_specs=[pl.BlockSpec((1, CHUNK, 3 * gw), fwd),
                  pl.BlockSpec((1, CHUNK, LANE), gate_map(fwd)),
                  pl.BlockSpec((1, CHUNK, 3 * gw), bwd),
                  pl.BlockSpec((1, CHUNK, LANE), gate_map(bwd)),
                  pl.BlockSpec((1, LANE), lambda bi, s: (0, 0)),
                  pl.BlockSpec((1, LANE), lambda bi, s: (0, 0))],
        out_specs=[pl.BlockSpec((1, CHUNK, gw), fwd), pl.BlockSpec((1, CHUNK, gw), bwd)],
        out_shape=[jax.ShapeDtypeStruct((b, t, gw), F32)] * 2,
        scratch_shapes=[pltpu.VMEM((2 * g, HEAD_DIM, HEAD_DIM), F32)],
        compiler_params=_params(("parallel", "arbitrary")),
    )(qkv, u, qkv, u, a_log_row, dt_row)


def _mlstm_scan_kernel(qf_ref, kf_ref, vf_ref, gf_ref, qb_ref, kb_ref, vb_ref, gb_ref, bias_ref,
                       of_ref, ob_ref, c_ref, n_ref, m_ref, *, g):
    @pl.when(pl.program_id(1) == 0)
    def _():
        c_ref[...] = jnp.zeros_like(c_ref)
        n_ref[...] = jnp.zeros_like(n_ref)
        m_ref[...] = jnp.zeros_like(m_ref)

    streams = range(2 * g)
    fwd, incl, q, k, v, b_b, ip_b = [], [], [], [], [], [], []
    dirs = ((qf_ref, kf_ref, vf_ref, gf_ref), (qb_ref, kb_ref, vb_ref, gb_ref))
    for d, (q_ref, k_ref, v_ref, gt_ref) in enumerate(dirs):
        m_incl = _chunk_masks(d == 0)[0]
        gates = gt_ref[0] + bias_ref[...]
        logsig = jnp.minimum(gates, 0.0) - jnp.log(1.0 + jnp.exp(-jnp.abs(gates)))
        b_all = _chunk_cumsum(logsig, m_incl)
        for h in range(g):
            c = d * g + h
            sl = slice(h * HEAD_DIM, (h + 1) * HEAD_DIM)
            fwd.append(d == 0)
            incl.append(m_incl)
            q.append(q_ref[0, :, sl] * (ML_DQK ** -0.5))
            k.append(k_ref[0, :, sl])
            v.append(v_ref[0, :, sl])
            b_b.append(_bcast(b_all[:, 2 * g + c:2 * g + c + 1]))
            ip_b.append(_bcast(gates[:, c:c + 1]))

    dmat = [jnp.where(incl[i], _pair_diff(b_b[i]) + ip_b[i].T[:CHUNK, :], -jnp.inf) for i in streams]
    qk = [_dot_nt(q[i], k[i]) for i in streams]
    cm = [c_ref[i] for i in streams]
    nrow = [n_ref[i] for i in streams]
    mrow = [m_ref[i] for i in streams]
    b_end = [b_b[i][CHUNK - 1:CHUNK, :] if fwd[i] else b_b[i][0:1, :] for i in streams]
    w_b = [b_end[i] - b_b[i] + ip_b[i] for i in streams]
    inter = [b_b[i] + mrow[i] for i in streams]
    m_t = [jnp.maximum(inter[i], jnp.max(dmat[i], axis=1, keepdims=True)) for i in streams]
    s_inter = [jnp.exp(inter[i] - m_t[i]) for i in streams]
    p = [jnp.exp(dmat[i] - m_t[i][:, :CHUNK]) * qk[i] for i in streams]
    num = [s_inter[i] * _dot(q[i], cm[i]) + _dot(p[i], v[i]) for i in streams]
    den = [s_inter[i] * jnp.sum(q[i] * nrow[i], axis=1, keepdims=True) + jnp.sum(p[i], axis=1, keepdims=True)
           for i in streams]
    hh = [num[i] / jnp.maximum(jnp.abs(den[i]), jnp.exp(-m_t[i])) for i in streams]
    m_new = [jnp.maximum(b_end[i] + mrow[i], jnp.max(w_b[i], axis=0, keepdims=True)) for i in streams]
    s_old = [jnp.exp(b_end[i] + mrow[i] - m_new[i]) for i in streams]
    wk = [jnp.exp(w_b[i] - m_new[i]) * k[i] for i in streams]
    cm_new = [s_old[i] * cm[i] + _dot_tn(wk[i], v[i]) for i in streams]
    n_new = [s_old[i] * nrow[i] + jnp.sum(wk[i], axis=0, keepdims=True) for i in streams]
    for i in streams:
        o_ref = of_ref if fwd[i] else ob_ref
        h = i % g
        o_ref[0, :, h * HEAD_DIM:(h + 1) * HEAD_DIM] = hh[i]
        c_ref[i] = cm_new[i]
        n_ref[i] = n_new[i]
        m_ref[i] = m_new[i]


def _mlstm_scan(u, bias_row, g, ctx_len, off):
    b, t, _ = u.shape
    n_all, n_ctx = t // CHUNK, ctx_len // CHUNK
    fwd, bwd = _scan_chunk_maps(n_ctx, n_all)
    gw = g * HEAD_DIM

    def col_map(m, blk):
        return lambda bi, s: (m(bi, s)[0], m(bi, s)[1], blk)

    def specs(m):
        return [pl.BlockSpec((1, CHUNK, gw), col_map(m, off["ml_q"] // gw)),
                pl.BlockSpec((1, CHUNK, gw), col_map(m, off["ml_k"] // gw)),
                pl.BlockSpec((1, CHUNK, gw), col_map(m, off["ml_v"] // gw)),
                pl.BlockSpec((1, CHUNK, LANE), col_map(m, off["ml_if"] // LANE))]

    return pl.pallas_call(
        functools.partial(_mlstm_scan_kernel, g=g),
        name="mlstm_scan",
        grid=(b, n_all),
        in_specs=specs(fwd) + specs(bwd) + [pl.BlockSpec((1, LANE), lambda bi, s: (0, 0))],
        out_specs=[pl.BlockSpec((1, CHUNK, gw), fwd), pl.BlockSpec((1, CHUNK, gw), bwd)],
        out_shape=[jax.ShapeDtypeStruct((b, t, gw), F32)] * 2,
        scratch_shapes=[pltpu.VMEM((2 * g, HEAD_DIM, HEAD_DIM), F32),
                        pltpu.VMEM((2 * g, 1, HEAD_DIM), F32),
                        pltpu.VMEM((2 * g, 1, HEAD_DIM), F32)],
        compiler_params=_params(("parallel", "arbitrary")),
    )(u, u, u, u, u, u, u, u, bias_row)


def _rope(x, c, s_up, s_dn, shift):
    return x * c + pltpu.roll(x, LANE - shift, 1) * s_up + pltpu.roll(x, shift, 1) * s_dn


def _gqa_prep_kernel(q_ref, k_ref, v_ref, c_ref, su_ref, sd_ref, n_ref, qo_ref, ko_ref, vo_ref, *, g):
    c, su, sd = c_ref[...], su_ref[...], sd_ref[...]
    scale = HEAD_DIM ** -0.5
    for h in range(g):
        sl = slice(h * HEAD_DIM, (h + 1) * HEAD_DIM)
        qo_ref[0, :, sl] = (_rope(_rms(q_ref[0, :, sl], n_ref[0:1, :]), c, su, sd, 32) * scale).astype(BF16)
    for h in range(g // 2):
        sl = slice(h * HEAD_DIM, (h + 1) * HEAD_DIM)
        ko_ref[0, :, sl] = _rope(_rms(k_ref[0, :, sl], n_ref[1:2, :]), c, su, sd, 32).astype(BF16)
    vo_ref[0] = v_ref[0].astype(BF16)


def _gqa_prep(u, tables, qk_norm, g, off):
    b, t, _ = u.shape
    tm = _row_tile(t)
    gw, kw = g * HEAD_DIM, g * HEAD_DIM // 2
    tab = pl.BlockSpec((tm, LANE), lambda bi, i: (i, 0))
    return pl.pallas_call(
        functools.partial(_gqa_prep_kernel, g=g),
        name="gqa_prep",
        grid=(b, t // tm),
        in_specs=[pl.BlockSpec((1, tm, gw), lambda bi, i: (bi, i, off["gqa_q"] // gw)),
                  pl.BlockSpec((1, tm, kw), lambda bi, i: (bi, i, off["gqa_k"] // kw)),
                  pl.BlockSpec((1, tm, kw), lambda bi, i: (bi, i, off["gqa_v"] // kw)),
                  tab, tab, tab,
                  pl.BlockSpec((2, HEAD_DIM), lambda bi, i: (0, 0))],
        out_specs=[pl.BlockSpec((1, tm, gw), lambda bi, i: (bi, i, 0)),
                   pl.BlockSpec((1, tm, kw), lambda bi, i: (bi, i, 0)),
                   pl.BlockSpec((1, tm, kw), lambda bi, i: (bi, i, 0))],
        out_shape=[jax.ShapeDtypeStruct((b, t, gw), BF16),
                   jax.ShapeDtypeStruct((b, t, kw), BF16),
                   jax.ShapeDtypeStruct((b, t, kw), BF16)],
        compiler_params=_params(("parallel", "parallel")),
    )(u, u, u, *tables, qk_norm)


def _mla_prep_kernel(x_ref, kpe_ref, c_ref, su_ref, sd_ref, qn_ref, kvn_ref, wq_ref, wk_ref, wv_ref,
                     qo_ref, ko_ref, vo_ref, *, g):
    c, su, sd = c_ref[...], su_ref[...], sd_ref[...]
    scale = (MLA_NOPE + MLA_ROPE) ** -0.5
    x = x_ref[0]
    cq = _rms(x[:, :MLA_Q_RANK], qn_ref[...])
    ckv = _rms(x[:, MLA_Q_RANK:], kvn_ref[...])
    q = _dot(cq, wq_ref[...])
    kn = _dot(ckv, wk_ref[...])
    vo_ref[0] = _dot(ckv, wv_ref[...]).astype(BF16)
    kpe = _rope(kpe_ref[0], c, su, sd, 16).astype(BF16)
    for h in range(g):
        lo = h * MLA_QK_PAD
        qo_ref[0, :, lo:lo + LANE] = (q[:, lo:lo + LANE] * scale).astype(BF16)
        qo_ref[0, :, lo + LANE:lo + 2 * LANE] = (
            _rope(q[:, lo + LANE:lo + 2 * LANE], c, su, sd, 16) * scale).astype(BF16)
        ko_ref[0, :, lo:lo + LANE] = kn[:, h * LANE:(h + 1) * LANE].astype(BF16)
        ko_ref[0, :, lo + LANE:lo + 2 * LANE] = kpe


def _mla_prep(u, tables, q_norm, kv_norm, wq, wk, wv, g, off):
    b, t, _ = u.shape
    tm = _row_tile(t)
    cw = MLA_Q_RANK + MLA_KV_RANK
    tab = pl.BlockSpec((tm, LANE), lambda bi, i: (i, 0))

    def full(a):
        return pl.BlockSpec(a.shape, lambda bi, i: (0,) * a.ndim)

    return pl.pallas_call(
        functools.partial(_mla_prep_kernel, g=g),
        name="mla_prep",
        grid=(b, t // tm),
        in_specs=[pl.BlockSpec((1, tm, cw), lambda bi, i: (bi, i, off["mla_cqkv"] // cw)),
                  pl.BlockSpec((1, tm, LANE), lambda bi, i: (bi, i, off["mla_kpe"] // LANE)),
                  tab, tab, tab, full(q_norm), full(kv_norm), full(wq), full(wk), full(wv)],
        out_specs=[pl.BlockSpec((1, tm, g * MLA_QK_PAD), lambda bi, i: (bi, i, 0)),
                   pl.BlockSpec((1, tm, g * MLA_QK_PAD), lambda bi, i: (bi, i, 0)),
                   pl.BlockSpec((1, tm, g * MLA_DV), lambda bi, i: (bi, i, 0))],
        out_shape=[jax.ShapeDtypeStruct((b, t, g * MLA_QK_PAD), BF16),
                   jax.ShapeDtypeStruct((b, t, g * MLA_QK_PAD), BF16),
                   jax.ShapeDtypeStruct((b, t, g * MLA_DV), BF16)],
        compiler_params=_params(("parallel", "parallel")),
    )(u, u, *tables, q_norm, kv_norm, wq, wk, wv)


HEADS_PER_STEP = 2


def _attn_kernel(q_ref, k_ref, v_ref, o_ref, *, ctx_len, tq, rep, dq, dv):
    heads = range(HEADS_PER_STEP)

    def attend(nk):
        s = [lax.dot_general(q_ref[0, :, j * dq:(j + 1) * dq], k_ref[0, :nk, (j // rep) * dq:(j // rep + 1) * dq],
                             (((1,), (1,)), ((), ())), preferred_element_type=F32) for j in heads]
        p = [jnp.exp(s[j] - jnp.max(s[j], axis=-1, keepdims=True)) for j in heads]
        l = [jnp.sum(p[j], axis=-1, keepdims=True) for j in heads]
        o = [jnp.dot(p[j].astype(BF16), v_ref[0, :nk, (j // rep) * dv:(j // rep + 1) * dv],
                     preferred_element_type=F32) for j in heads]
        for j in heads:
            o_ref[0, :, j * dv:(j + 1) * dv] = (o[j] / l[j]).astype(o_ref.dtype)

    is_ctx = pl.program_id(2) * tq < ctx_len

    @pl.when(is_ctx)
    def _():
        attend(ctx_len)

    @pl.when(jnp.logical_not(is_ctx))
    def _():
        attend(k_ref.shape[1])


def _attention(q, k, v, heads, kv_heads, ctx_len):
    b, t, _ = q.shape
    dq, dv = q.shape[2] // heads, v.shape[2] // kv_heads
    rep = heads // kv_heads
    hp = HEADS_PER_STEP
    assert heads % hp == 0 and (hp % rep == 0 or rep % hp == 0)
    kvp = max(hp // rep, 1)
    tq = math.gcd(256, ctx_len)
    return pl.pallas_call(
        functools.partial(_attn_kernel, ctx_len=ctx_len, tq=tq, rep=rep, dq=dq, dv=dv),
        name="attn",
        grid=(b, heads // hp, t // tq),
        in_specs=[pl.BlockSpec((1, tq, hp * dq), lambda bi, h, i: (bi, i, h)),
                  pl.BlockSpec((1, t, kvp * dq), lambda bi, h, i: (bi, 0, h * hp // (rep * kvp))),
                  pl.BlockSpec((1, t, kvp * dv), lambda bi, h, i: (bi, 0, h * hp // (rep * kvp)))],
        out_specs=pl.BlockSpec((1, tq, hp * dv), lambda bi, h, i: (bi, i, h)),
        out_shape=jax.ShapeDtypeStruct((b, t, heads * dv), BF16),
        compiler_params=_params(("parallel", "parallel", "parallel"), VMEM_LIMIT),
    )(q, k, v)


def _head_norm(x, w_row, g):
    parts = []
    for h in range(g):
        xs = x[:, h * HEAD_DIM:(h + 1) * HEAD_DIM]
        parts.append(xs * lax.rsqrt(jnp.mean(xs * xs, axis=-1, keepdims=True) + NORM_EPS))
    return jnp.concatenate(parts, axis=-1) * w_row


def _outproj_kernel(af_ref, ab_ref, ga_ref, yb_ref, cf_ref, cb_ref, gc_ref, yd_ref, h_ref, mod_ref, g_ref,
                    na_ref, nc_ref, w_ref, o_ref, *, g, ctx_len, tm):
    row = pl.program_id(1) * tm + lax.broadcasted_iota(jnp.int32, (tm, 1), 0)
    ga = ga_ref[0]
    ya = _head_norm(af_ref[0] + ab_ref[0], na_ref[...], g) * (ga * _sigmoid(ga))
    yc = _head_norm(cf_ref[0] + cb_ref[0], nc_ref[...], g) * _sigmoid(gc_ref[0])
    y = jnp.concatenate([ya.astype(BF16), yb_ref[0], yc.astype(BF16), yd_ref[0]], axis=-1)
    z = jnp.dot(y, w_ref[...], preferred_element_type=F32)
    o_ref[0] = h_ref[0] + _mod_rows(mod_ref, 2, row, ctx_len) * _rms(z, g_ref[1:2, :])


def _outproj(oa, ob, oc, od, u, h, mod, g4, na_row, nc_row, w, g, ctx_len, off):
    b, t, d = h.shape
    tm = _row_tile(t)
    gw = g * HEAD_DIM

    def rows(width, blk=0):
        return pl.BlockSpec((1, tm, width), lambda bi, i: (bi, i, blk))

    return pl.pallas_call(
        functools.partial(_outproj_kernel, g=g, ctx_len=ctx_len, tm=tm),
        name="outproj",
        grid=(b, t // tm),
        in_specs=[rows(gw), rows(gw), rows(gw, off["gdn_gate"] // gw), rows(gw),
                  rows(gw), rows(gw), rows(gw, off["ml_og"] // gw), rows(gw),
                  rows(d),
                  pl.BlockSpec((1, 2, 6, d), lambda bi, i: (bi, 0, 0, 0)),
                  pl.BlockSpec((4, d), lambda bi, i: (0, 0)),
                  pl.BlockSpec((1, gw), lambda bi, i: (0, 0)),
                  pl.BlockSpec((1, gw), lambda bi, i: (0, 0)),
                  pl.BlockSpec(w.shape, lambda bi, i: (0, 0))],
        out_specs=rows(d),
        out_shape=jax.ShapeDtypeStruct((b, t, d), F32),
        compiler_params=_params(("parallel", "parallel"), VMEM_LIMIT),
    )(oa[0], oa[1], u, ob, oc[0], oc[1], u, od, h, mod, g4, na_row, nc_row, w)


def _ffn_kernel(h_ref, mod_ref, g_ref, w1_ref, w2_ref, o_ref, xn_ref, acc_ref, *, ctx_len, tm):
    k = pl.program_id(2)
    row = pl.program_id(1) * tm + lax.broadcasted_iota(jnp.int32, (tm, 1), 0)

    @pl.when(k == 0)
    def _():
        y = _rms(h_ref[0], g_ref[2:3, :])
        y = y * (1.0 + _mod_rows(mod_ref, 4, row, ctx_len)) + _mod_rows(mod_ref, 3, row, ctx_len)
        xn_ref[...] = y.astype(BF16)
        acc_ref[...] = jnp.zeros_like(acc_ref)

    hid = jnp.maximum(jnp.dot(xn_ref[...], w1_ref[...], preferred_element_type=F32), 0.0)
    acc_ref[...] += jnp.dot((hid * hid).astype(BF16), w2_ref[...], preferred_element_type=F32)

    @pl.when(k == pl.num_programs(2) - 1)
    def _():
        o_ref[0] = h_ref[0] + _mod_rows(mod_ref, 5, row, ctx_len) * _rms(acc_ref[...], g_ref[3:4, :])


def _ffn(h, mod, g4, w1, w2, ctx_len):
    b, t, d = h.shape
    ff = w1.shape[1]
    tm = _row_tile(t)
    tf = _pick_tile(ff, (1024, 512, 256, 128))
    return pl.pallas_call(
        functools.partial(_ffn_kernel, ctx_len=ctx_len, tm=tm),
        name="ffn",
        grid=(b, t // tm, ff // tf),
        in_specs=[pl.BlockSpec((1, tm, d), lambda bi, i, k: (bi, i, 0)),
                  pl.BlockSpec((1, 2, 6, d), lambda bi, i, k: (bi, 0, 0, 0)),
                  pl.BlockSpec((4, d), lambda bi, i, k: (0, 0)),
                  pl.BlockSpec((d, tf), lambda bi, i, k: (0, k)),
                  pl.BlockSpec((tf, d), lambda bi, i, k: (k, 0))],
        out_specs=pl.BlockSpec((1, tm, d), lambda bi, i, k: (bi, i, 0)),
        out_shape=jax.ShapeDtypeStruct((b, t, d), F32),
        scratch_shapes=[pltpu.VMEM((tm, d), BF16), pltpu.VMEM((tm, d), F32)],
        compiler_params=_params(("parallel", "parallel", "arbitrary"), VMEM_LIMIT),
    )(h, mod, g4, w1, w2)


def _rope_tables(seq, ctx_len, dim):
    quarter = dim // 4
    t = jnp.arange(seq, dtype=jnp.int32)
    inv_freq = ROPE_BASE ** (-jnp.arange(quarter, dtype=F32) / quarter)
    ang_r = (t // GRID_W).astype(F32)[:, None] * inv_freq
    ang_c = (t % GRID_W).astype(F32)[:, None] * inv_freq
    zero = jnp.zeros_like(ang_r)
    cos = jnp.concatenate([jnp.cos(ang_r)] * 2 + [jnp.cos(ang_c)] * 2, axis=-1)
    s_up = jnp.concatenate([-jnp.sin(ang_r), zero, -jnp.sin(ang_c), zero], axis=-1)
    s_dn = jnp.concatenate([zero, jnp.sin(ang_r), zero, jnp.sin(ang_c)], axis=-1)

    def finish(tab, fill):
        tab = jnp.pad(tab, ((0, 0), (0, LANE - dim)), constant_values=fill)
        return jnp.pad(tab, ((ctx_len, 0), (0, 0)), constant_values=fill)

    return finish(cos, 1.0), finish(s_up, 0.0), finish(s_dn, 0.0)


def _lane_row(v):
    v = v.reshape(1, -1)
    return jnp.pad(v, ((0, 0), (0, LANE - v.shape[1])))


def kernel(x, c, ctx, c_ctx, w_mod, b_mod, g_norm, w_in, gdn_conv, gdn_a_log, gdn_dt_bias, gdn_norm,
           gqa_qk_norm, mlstm_gate_bias, mlstm_norm, mla_q_norm, mla_kv_norm, mla_w_qb, mla_w_kvb,
           w_out, w_ff1, w_ff2):
    bsz, seq, d = x.shape
    ctx_len = ctx.shape[1]
    depth = w_in.shape[0]
    g = w_out.shape[1] // (4 * HEAD_DIM)
    assert ctx_len % CHUNK == 0 and seq % CHUNK == 0 and seq % GRID_W == 0
    off, _ = _layout(g)
    cols, n_pad, n_src = _packed_columns(g)
    assert n_src == w_in.shape[2]

    cols = jnp.asarray(cols, jnp.int32)
    w_in_p = jnp.where(cols >= 0, jnp.take(w_in, jnp.maximum(cols, 0), axis=2), 0.0).astype(BF16)
    wq = mla_w_qb.reshape(depth, MLA_Q_RANK, g, MLA_NOPE + MLA_ROPE)
    wq = jnp.pad(wq, ((0, 0), (0, 0), (0, 0), (0, MLA_QK_PAD - MLA_NOPE - MLA_ROPE)))
    wq = wq.reshape(depth, MLA_Q_RANK, g * MLA_QK_PAD).astype(BF16)
    wkv = mla_w_kvb.reshape(depth, MLA_KV_RANK, g, MLA_NOPE + MLA_DV)
    wk = wkv[..., :MLA_NOPE].reshape(depth, MLA_KV_RANK, g * MLA_NOPE).astype(BF16)
    wv = wkv[..., MLA_NOPE:].reshape(depth, MLA_KV_RANK, g * MLA_DV).astype(BF16)
    w_out_b, w_ff1_b, w_ff2_b = w_out.astype(BF16), w_ff1.astype(BF16), w_ff2.astype(BF16)
    gqa_tab = _rope_tables(seq, ctx_len, HEAD_DIM)
    mla_tab = _rope_tables(seq, ctx_len, MLA_ROPE)

    n_rows = -(-(bsz + 1) // 8) * 8
    c_rows = jnp.concatenate([c, c_ctx[None, :], jnp.zeros((n_rows - bsz - 1, d), F32)], axis=0)
    mod_all = _mod_all(c_rows, w_mod, b_mod).reshape(depth, n_rows, 6, d)
    mod_all = jnp.stack([jnp.broadcast_to(mod_all[:, bsz:bsz + 1], (depth, bsz, 6, d)), mod_all[:, :bsz]], axis=2)

    h = jnp.concatenate([ctx, x], axis=1)
    for l in range(depth):
        mod, g4 = mod_all[l], g_norm[l]
        u = _inproj(h, mod, g4, w_in_p[l], ctx_len)
        qkv = _gdn_prep(u, gdn_conv[l], g, ctx_len, off)
        oa = _gdn_scan(qkv, u, _lane_row(gdn_a_log[l]), _lane_row(gdn_dt_bias[l]), g, ctx_len, off)
        qn, kn, vn = _gqa_prep(u, gqa_tab, gqa_qk_norm[l], g, off)
        ob = _attention(qn, kn, vn, g, g // 2, ctx_len)
        oc = _mlstm_scan(u, _lane_row(mlstm_gate_bias[l]), g, ctx_len, off)
        qm, km, vm = _mla_prep(u, mla_tab, mla_q_norm[l].reshape(1, -1), mla_kv_norm[l].reshape(1, -1),
                               wq[l], wk[l], wv[l], g, off)
        od = _attention(qm, km, vm, g, g, ctx_len)
        h = _outproj(oa, ob, oc, od, u, h, mod, g4, jnp.tile(gdn_norm[l], g).reshape(1, -1),
                     mlstm_norm[l].reshape(1, -1), w_out_b[l], g, ctx_len, off)
        h = _ffn(h, mod, g4, w_ff1_b[l], w_ff2_b[l], ctx_len)
    return h[:, ctx_len:, :]
```

```python
import functools
import math

import jax
import jax.numpy as jnp
from jax import lax
from jax.experimental import pallas as pl
from jax.experimental.pallas import tpu as pltpu

F32 = jnp.float32
BF16 = jnp.bfloat16

LANE = 128
HEAD_DIM = 128
CHUNK = 64
GRID_W = 64
ROPE_BASE = 10000.0
NORM_EPS = 1e-6
ML_DQK = 64
MLA_Q_RANK = 384
MLA_KV_RANK = 128
MLA_NOPE = 128
MLA_ROPE = 64
MLA_DV = 128
MLA_QK_PAD = 256
VMEM_LIMIT = 56 * 1024 * 1024


def _dot(a, b):
    return jnp.dot(a.astype(BF16), b.astype(BF16), preferred_element_type=F32)


def _dot_nt(a, b):
    return lax.dot_general(a.astype(BF16), b.astype(BF16), (((1,), (1,)), ((), ())),
                           preferred_element_type=F32)


def _dot_tn(a, b):
    return lax.dot_general(a.astype(BF16), b.astype(BF16), (((0,), (0,)), ((), ())),
                           preferred_element_type=F32)


def _sigmoid(x):
    return 1.0 / (1.0 + jnp.exp(-x))


def _softplus(x):
    return jnp.maximum(x, 0.0) + jnp.log(1.0 + jnp.exp(-jnp.abs(x)))


def _rms(x, w):
    return x * lax.rsqrt(jnp.mean(x * x, axis=-1, keepdims=True) + NORM_EPS) * w


def _params(sem, vmem=None):
    return pltpu.CompilerParams(dimension_semantics=sem, vmem_limit_bytes=vmem)


def _layout(g):
    gw = g * HEAD_DIM
    order = [("mla_cqkv", MLA_Q_RANK + MLA_KV_RANK),
             ("gdn_q", gw), ("gdn_k", gw), ("gdn_v", gw), ("gdn_gate", gw),
             ("ml_q", gw), ("ml_k", gw), ("ml_v", gw), ("ml_og", gw), ("gqa_q", gw),
             ("gqa_k", gw // 2), ("gqa_v", gw // 2),
             ("gdn_ab", LANE), ("ml_if", LANE), ("mla_kpe", LANE)]
    off, pos = {}, 0
    for name, width in order:
        assert pos % width == 0, (name, pos, width)
        off[name] = pos
        pos += width
    return off, pos


def _packed_columns(g):
    gw = g * HEAD_DIM
    off, total = _layout(g)
    n_pad = -(-total // 512) * 512
    idx = [-1] * n_pad

    def put(name, src0, n, dst0=0):
        for t in range(n):
            idx[off[name] + dst0 + t] = src0 + t

    p = 0
    put("gdn_q", p, gw); p += gw
    put("gdn_k", p, gw); p += gw
    put("gdn_v", p, gw); p += gw
    put("gdn_gate", p, gw); p += gw
    put("gdn_ab", p, 4 * g); p += 4 * g
    put("gqa_q", p, gw); p += gw
    put("gqa_k", p, gw // 2); p += gw // 2
    put("gqa_v", p, gw // 2); p += gw // 2
    for name in ("ml_q", "ml_k"):
        for h in range(g):
            put(name, p + h * ML_DQK, ML_DQK, h * HEAD_DIM)
        p += g * ML_DQK
    put("ml_v", p, gw); p += gw
    put("ml_og", p, gw); p += gw
    put("ml_if", p, 4 * g); p += 4 * g
    put("mla_cqkv", p, MLA_Q_RANK + MLA_KV_RANK); p += MLA_Q_RANK + MLA_KV_RANK
    put("mla_kpe", p, MLA_ROPE); p += MLA_ROPE
    return idx, n_pad, p


def _pick_tile(n, candidates):
    for c in candidates:
        if n % c == 0:
            return c
    raise ValueError(f"no tile for {n} in {candidates}")


def _row_tile(t, limit=640):
    for c in range(limit - limit % 16, 15, -16):
        if t % c == 0:
            return c
    raise ValueError(f"no row tile for {t}")


def _mod_kernel(c_ref, w_ref, b_ref, o_ref):
    c = c_ref[...]
    o_ref[0] = _dot(c * _sigmoid(c), w_ref[0]) + b_ref[0]


def _mod_all(c_rows, w_mod, b_mod):
    depth, d, n = w_mod.shape
    r = c_rows.shape[0]
    tn = _pick_tile(n, (1024, 512, 256, 128))
    return pl.pallas_call(
        _mod_kernel,
        name="mod",
        grid=(depth, n // tn),
        in_specs=[pl.BlockSpec((r, d), lambda l, j: (0, 0)),
                  pl.BlockSpec((1, d, tn), lambda l, j: (l, 0, j)),
                  pl.BlockSpec((1, 1, tn), lambda l, j: (l, 0, j))],
        out_specs=pl.BlockSpec((1, r, tn), lambda l, j: (l, 0, j)),
        out_shape=jax.ShapeDtypeStruct((depth, r, n), F32),
        compiler_params=_params(("parallel", "parallel"), VMEM_LIMIT),
    )(c_rows, w_mod, b_mod.reshape(depth, 1, n))


def _mod_rows(mod_ref, idx, row, ctx_len):
    return jnp.where(row < ctx_len, mod_ref[0, 0, idx:idx + 1, :], mod_ref[0, 1, idx:idx + 1, :])


def _prenorm_kernel(h_ref, mod_ref, g_ref, o_ref, *, ctx_len, tm):
    row = pl.program_id(1) * tm + lax.broadcasted_iota(jnp.int32, (tm, 1), 0)
    y = _rms(h_ref[0], g_ref[0:1, :])
    y = y * (1.0 + _mod_rows(mod_ref, 1, row, ctx_len)) + _mod_rows(mod_ref, 0, row, ctx_len)
    o_ref[0] = y.astype(BF16)


def _inproj_kernel(x_ref, w_ref, o_ref):
    o_ref[0] = jnp.dot(x_ref[0], w_ref[...], preferred_element_type=F32)


def _inproj(h, mod, g4, w, ctx_len):
    b, t, d = h.shape
    n = w.shape[1]
    tm = _row_tile(t)
    xn = pl.pallas_call(
        functools.partial(_prenorm_kernel, ctx_len=ctx_len, tm=tm),
        name="prenorm",
        grid=(b, t // tm),
        in_specs=[pl.BlockSpec((1, tm, d), lambda bi, i: (bi, i, 0)),
                  pl.BlockSpec((1, 2, 6, d), lambda bi, i: (bi, 0, 0, 0)),
                  pl.BlockSpec((4, d), lambda bi, i: (0, 0))],
        out_specs=pl.BlockSpec((1, tm, d), lambda bi, i: (bi, i, 0)),
        out_shape=jax.ShapeDtypeStruct((b, t, d), BF16),
        compiler_params=_params(("parallel", "parallel"), VMEM_LIMIT),
    )(h, mod, g4)
    tm = _row_tile(t, 1280)
    tn = _pick_tile(n, (1536, 1024, 512))
    return pl.pallas_call(
        _inproj_kernel,
        name="inproj",
        grid=(n // tn, b, t // tm),
        in_specs=[pl.BlockSpec((1, tm, d), lambda j, bi, i: (bi, i, 0)),
                  pl.BlockSpec((d, tn), lambda j, bi, i: (0, j))],
        out_specs=pl.BlockSpec((1, tm, tn), lambda j, bi, i: (bi, i, j)),
        out_shape=jax.ShapeDtypeStruct((b, t, n), F32),
        compiler_params=_params(("parallel", "parallel", "parallel"), VMEM_LIMIT),
    )(xn, w)


def _gdn_prep_kernel(u_ref, w_ref, o_ref, *, g, ctx_len):
    j = pl.program_id(1)
    x = u_ref[0]
    t = x.shape[0]
    row = lax.broadcasted_iota(jnp.int32, (t, 1), 0)
    first = (row == 0) | (row == ctx_len)
    last = (row == ctx_len - 1) | (row == t - 1)
    x_prev = jnp.where(first, 0.0, pltpu.roll(x, 1, 0))
    x_next = jnp.where(last, 0.0, pltpu.roll(x, t - 1, 0))
    y = x_prev * w_ref[0:1, :] + x * w_ref[1:2, :] + x_next * w_ref[2:3, :]
    y = y * _sigmoid(y)
    yn = y * lax.rsqrt(jnp.sum(y * y, axis=-1, keepdims=True) + NORM_EPS)
    scale = jnp.where(j < g, HEAD_DIM ** -0.5, 1.0)
    o_ref[0] = jnp.where(j < 2 * g, yn * scale, y)


def _gdn_prep(u, conv_w, g, ctx_len, off):
    b, t, _ = u.shape
    blk0 = off["gdn_q"] // LANE
    return pl.pallas_call(
        functools.partial(_gdn_prep_kernel, g=g, ctx_len=ctx_len),
        name="gdn_prep",
        grid=(b, 3 * g),
        in_specs=[pl.BlockSpec((1, t, LANE), lambda bi, j: (bi, 0, blk0 + j)),
                  pl.BlockSpec((3, LANE), lambda bi, j: (0, j))],
        out_specs=pl.BlockSpec((1, t, LANE), lambda bi, j: (bi, 0, j)),
        out_shape=jax.ShapeDtypeStruct((b, t, 3 * g * LANE), F32),
        compiler_params=_params(("parallel", "parallel"), VMEM_LIMIT),
    )(u, conv_w)


def _chunk_masks(fwd):
    ri = lax.broadcasted_iota(jnp.int32, (CHUNK, CHUNK), 0)
    ci = lax.broadcasted_iota(jnp.int32, (CHUNK, CHUNK), 1)
    if fwd:
        return ri >= ci, ri > ci, ri == ci
    return ri <= ci, ri < ci, ri == ci


def _bcast(col):
    return jnp.broadcast_to(col, (CHUNK, LANE))


def _split3(x):
    x1 = x.astype(BF16)
    r = x - x1.astype(F32)
    x2 = r.astype(BF16)
    return x1, x2, (r - x2.astype(F32)).astype(BF16)


def _chunk_cumsum(x, incl):
    lmat = jnp.where(incl, 1.0, 0.0).astype(BF16)
    p1, p2, p3 = _split3(x)
    return (jnp.dot(lmat, p1, preferred_element_type=F32) + jnp.dot(lmat, p2, preferred_element_type=F32)
            + jnp.dot(lmat, p3, preferred_element_type=F32))


def _pair_diff(g_b):
    return g_b[:, :CHUNK] - g_b.T[:CHUNK, :]


def _tri_inverse_all(a_list, diag):
    ri = lax.broadcasted_iota(jnp.int32, (CHUNK, CHUNK), 0)
    ci = lax.broadcasted_iota(jnp.int32, (CHUNK, CHUNK), 1)
    eye = diag.astype(F32)
    ts = [jnp.where((ri >> 1) == (ci >> 1), eye - a, 0.0) for a in a_list]
    for lb in range(1, int(math.log2(CHUNK))):
        join = ((ri >> (lb + 1)) == (ci >> (lb + 1))) & ((ri >> lb) != (ci >> lb))
        us = [_dot(jnp.where(join, a, 0.0), t) for a, t in zip(a_list, ts)]
        ts = [t - _dot(t, u) for t, u in zip(ts, us)]
    return ts


def _gdn_scan_kernel(qf_ref, gf_ref, qb_ref, gb_ref, al_ref, dt_ref, of_ref, ob_ref, s_ref, *, g, nb):
    @pl.when(pl.program_id(1) == 0)
    def _():
        s_ref[...] = jnp.zeros_like(s_ref)

    gw = g * HEAD_DIM
    streams = range(nb * 2 * g)
    fwd, incl, strict, q, k, v, g_b, beta_b, dest = [], [], [], [], [], [], [], [], []
    diag = _chunk_masks(True)[2]
    for bb in range(nb):
        for d, (q_ref, gt_ref) in enumerate(((qf_ref, gf_ref), (qb_ref, gb_ref))):
            m_incl, m_strict, _ = _chunk_masks(d == 0)
            gates = gt_ref[bb]
            la = -jnp.exp(al_ref[...]) * _softplus(gates + dt_ref[...])
            g_all = _chunk_cumsum(la, m_incl)
            be = _sigmoid(gates)
            for h in range(g):
                c = d * g + h
                fwd.append(d == 0)
                incl.append(m_incl)
                strict.append(m_strict)
                dest.append((of_ref if d == 0 else ob_ref, bb, h))
                q.append(q_ref[bb, :, h * HEAD_DIM:(h + 1) * HEAD_DIM])
                k.append(q_ref[bb, :, gw + h * HEAD_DIM:gw + (h + 1) * HEAD_DIM])
                v.append(q_ref[bb, :, 2 * gw + h * HEAD_DIM:2 * gw + (h + 1) * HEAD_DIM])
                g_b.append(_bcast(g_all[:, c:c + 1]))
                beta_b.append(_bcast(be[:, 2 * g + c:2 * g + c + 1]))

    decay = [jnp.exp(jnp.where(incl[i], _pair_diff(g_b[i]), -jnp.inf)) for i in streams]
    kb = [k[i] * beta_b[i] for i in streams]
    kq = [_dot_nt(jnp.concatenate([kb[i], q[i]], axis=0), k[i]) for i in streams]
    a = [jnp.where(strict[i], kq[i][:CHUNK] * decay[i], 0.0) for i in streams]
    attn = [kq[i][CHUNK:] * decay[i] for i in streams]
    t = _tri_inverse_all(a, diag)
    eg = [jnp.exp(g_b[i]) for i in streams]
    vw = [_dot(t[i], jnp.concatenate([v[i] * beta_b[i], kb[i] * eg[i]], axis=1)) for i in streams]
    s = [s_ref[i] for i in streams]
    ws = [_dot(jnp.concatenate([vw[i][:, HEAD_DIM:], q[i] * eg[i]], axis=0), s[i]) for i in streams]
    v_new = [vw[i][:, :HEAD_DIM] - ws[i][:CHUNK] for i in streams]
    o = [ws[i][CHUNK:] + _dot(attn[i], v_new[i]) for i in streams]
    g_end = [g_b[i][CHUNK - 1:CHUNK, :] if fwd[i] else g_b[i][0:1, :] for i in streams]
    s_new = [s[i] * jnp.exp(g_end[i]) + _dot_tn(k[i] * jnp.exp(g_end[i] - g_b[i]), v_new[i]) for i in streams]
    for i in streams:
        o_ref, bb, h = dest[i]
        o_ref[bb, :, h * HEAD_DIM:(h + 1) * HEAD_DIM] = o[i]
        s_ref[i] = s_new[i]


SCAN_BATCH = 4


def _scan_batch(b):
    return math.gcd(b, SCAN_BATCH)


def _scan_chunk_maps(n_ctx, n_all):
    def fwd(bi, s):
        return (bi, s, 0)

    def bwd(bi, s):
        return (bi, jnp.where(s < n_ctx, n_ctx - 1 - s, n_all + n_ctx - 1 - s), 0)

    return fwd, bwd


def _gdn_scan(qkv, u, a_log_row, dt_row, g, ctx_len, off):
    b, t, _ = qkv.shape
    n_all, n_ctx = t // CHUNK, ctx_len // CHUNK
    fwd, bwd = _scan_chunk_maps(n_ctx, n_all)
    gblk = off["gdn_ab"] // LANE

    def gate_map(m):
        return lambda bi, s: (m(bi, s)[0], m(bi, s)[1], gblk)

    gw = g * HEAD_DIM
    nb = _scan_batch(b)
    return pl.pallas_call(
        functools.partial(_gdn_scan_kernel, g=g, nb=nb),
        name="gdn_scan",
        grid=(b // nb, n_all),
        in_specs=[pl.BlockSpec((nb, CHUNK, 3 * gw), fwd),
                  pl.BlockSpec((nb, CHUNK, LANE), gate_map(fwd)),
                  pl.BlockSpec((nb, CHUNK, 3 * gw), bwd),
                  pl.BlockSpec((nb, CHUNK, LANE), gate_map(bwd)),
                  pl.BlockSpec((1, LANE), lambda bi, s: (0, 0)),
                  pl.BlockSpec((1, LANE), lambda bi, s: (0, 0))],
        out_specs=[pl.BlockSpec((nb, CHUNK, gw), fwd), pl.BlockSpec((nb, CHUNK, gw), bwd)],
        out_shape=[jax.ShapeDtypeStruct((b, t, gw), F32)] * 2,
        scratch_shapes=[pltpu.VMEM((nb * 2 * g, HEAD_DIM, HEAD_DIM), F32)],
        compiler_params=_params(("parallel", "arbitrary")),
    )(qkv, u, qkv, u, a_log_row, dt_row)


def _mlstm_scan_kernel(qf_ref, kf_ref, vf_ref, gf_ref, qb_ref, kb_ref, vb_ref, gb_ref, bias_ref,
                       of_ref, ob_ref, c_ref, n_ref, m_ref, *, g, nb):
    @pl.when(pl.program_id(1) == 0)
    def _():
        c_ref[...] = jnp.zeros_like(c_ref)
        n_ref[...] = jnp.zeros_like(n_ref)
        m_ref[...] = jnp.zeros_like(m_ref)

    streams = range(nb * 2 * g)
    fwd, incl, q, k, v, b_b, ip_b, dest = [], [], [], [], [], [], [], []
    dirs = ((qf_ref, kf_ref, vf_ref, gf_ref), (qb_ref, kb_ref, vb_ref, gb_ref))
    for bb in range(nb):
        for d, (q_ref, k_ref, v_ref, gt_ref) in enumerate(dirs):
            m_incl = _chunk_masks(d == 0)[0]
            gates = gt_ref[bb] + bias_ref[...]
            logsig = jnp.minimum(gates, 0.0) - jnp.log(1.0 + jnp.exp(-jnp.abs(gates)))
            b_all = _chunk_cumsum(logsig, m_incl)
            for h in range(g):
                c = d * g + h
                sl = slice(h * HEAD_DIM, (h + 1) * HEAD_DIM)
                fwd.append(d == 0)
                incl.append(m_incl)
                dest.append((of_ref if d == 0 else ob_ref, bb, h))
                q.append(q_ref[bb, :, sl] * (ML_DQK ** -0.5))
                k.append(k_ref[bb, :, sl])
                v.append(v_ref[bb, :, sl])
                b_b.append(_bcast(b_all[:, 2 * g + c:2 * g + c + 1]))
                ip_b.append(_bcast(gates[:, c:c + 1]))

    dmat = [jnp.where(incl[i], _pair_diff(b_b[i]) + ip_b[i].T[:CHUNK, :], -jnp.inf) for i in streams]
    qk = [_dot_nt(q[i], k[i]) for i in streams]
    cm = [c_ref[i] for i in streams]
    nrow = [n_ref[i] for i in streams]
    mrow = [m_ref[i] for i in streams]
    b_end = [b_b[i][CHUNK - 1:CHUNK, :] if fwd[i] else b_b[i][0:1, :] for i in streams]
    w_b = [b_end[i] - b_b[i] + ip_b[i] for i in streams]
    inter = [b_b[i] + mrow[i] for i in streams]
    m_t = [jnp.maximum(inter[i], jnp.max(dmat[i], axis=1, keepdims=True)) for i in streams]
    s_inter = [jnp.exp(inter[i] - m_t[i]) for i in streams]
    p = [jnp.exp(dmat[i] - m_t[i][:, :CHUNK]) * qk[i] for i in streams]
    num = [s_inter[i] * _dot(q[i], cm[i]) + _dot(p[i], v[i]) for i in streams]
    den = [s_inter[i] * jnp.sum(q[i] * nrow[i], axis=1, keepdims=True) + jnp.sum(p[i], axis=1, keepdims=True)
           for i in streams]
    hh = [num[i] / jnp.maximum(jnp.abs(den[i]), jnp.exp(-m_t[i])) for i in streams]
    m_new = [jnp.maximum(b_end[i] + mrow[i], jnp.max(w_b[i], axis=0, keepdims=True)) for i in streams]
    s_old = [jnp.exp(b_end[i] + mrow[i] - m_new[i]) for i in streams]
    wk = [jnp.exp(w_b[i] - m_new[i]) * k[i] for i in streams]
    cm_new = [s_old[i] * cm[i] + _dot_tn(wk[i], v[i]) for i in streams]
    n_new = [s_old[i] * nrow[i] + jnp.sum(wk[i], axis=0, keepdims=True) for i in streams]
    for i in streams:
        o_ref, bb, h = dest[i]
        o_ref[bb, :, h * HEAD_DIM:(h + 1) * HEAD_DIM] = hh[i]
        c_ref[i] = cm_new[i]
        n_ref[i] = n_new[i]
        m_ref[i] = m_new[i]


def _mlstm_scan(u, bias_row, g, ctx_len, off):
    b, t, _ = u.shape
    n_all, n_ctx = t // CHUNK, ctx_len // CHUNK
    fwd, bwd = _scan_chunk_maps(n_ctx, n_all)
    gw = g * HEAD_DIM

    def col_map(m, blk):
        return lambda bi, s: (m(bi, s)[0], m(bi, s)[1], blk)

    nb = _scan_batch(b)

    def specs(m):
        return [pl.BlockSpec((nb, CHUNK, gw), col_map(m, off["ml_q"] // gw)),
                pl.BlockSpec((nb, CHUNK, gw), col_map(m, off["ml_k"] // gw)),
                pl.BlockSpec((nb, CHUNK, gw), col_map(m, off["ml_v"] // gw)),
                pl.BlockSpec((nb, CHUNK, LANE), col_map(m, off["ml_if"] // LANE))]

    return pl.pallas_call(
        functools.partial(_mlstm_scan_kernel, g=g, nb=nb),
        name="mlstm_scan",
        grid=(b // nb, n_all),
        in_specs=specs(fwd) + specs(bwd) + [pl.BlockSpec((1, LANE), lambda bi, s: (0, 0))],
        out_specs=[pl.BlockSpec((nb, CHUNK, gw), fwd), pl.BlockSpec((nb, CHUNK, gw), bwd)],
        out_shape=[jax.ShapeDtypeStruct((b, t, gw), F32)] * 2,
        scratch_shapes=[pltpu.VMEM((nb * 2 * g, HEAD_DIM, HEAD_DIM), F32),
                        pltpu.VMEM((nb * 2 * g, 1, HEAD_DIM), F32),
                        pltpu.VMEM((nb * 2 * g, 1, HEAD_DIM), F32)],
        compiler_params=_params(("parallel", "arbitrary")),
    )(u, u, u, u, u, u, u, u, bias_row)


def _rope(x, c, s_up, s_dn, shift):
    return x * c + pltpu.roll(x, LANE - shift, 1) * s_up + pltpu.roll(x, shift, 1) * s_dn


def _gqa_prep_kernel(q_ref, k_ref, v_ref, c_ref, su_ref, sd_ref, n_ref, qo_ref, ko_ref, vo_ref, *, g):
    c, su, sd = c_ref[...], su_ref[...], sd_ref[...]
    scale = HEAD_DIM ** -0.5
    for h in range(g):
        sl = slice(h * HEAD_DIM, (h + 1) * HEAD_DIM)
        qo_ref[0, :, sl] = (_rope(_rms(q_ref[0, :, sl], n_ref[0:1, :]), c, su, sd, 32) * scale).astype(BF16)
    for h in range(g // 2):
        sl = slice(h * HEAD_DIM, (h + 1) * HEAD_DIM)
        ko_ref[0, :, sl] = _rope(_rms(k_ref[0, :, sl], n_ref[1:2, :]), c, su, sd, 32).astype(BF16)
    vo_ref[0] = v_ref[0].astype(BF16)


def _gqa_prep(u, tables, qk_norm, g, off):
    b, t, _ = u.shape
    tm = _row_tile(t)
    gw, kw = g * HEAD_DIM, g * HEAD_DIM // 2
    tab = pl.BlockSpec((tm, LANE), lambda bi, i: (i, 0))
    return pl.pallas_call(
        functools.partial(_gqa_prep_kernel, g=g),
        name="gqa_prep",
        grid=(b, t // tm),
        in_specs=[pl.BlockSpec((1, tm, gw), lambda bi, i: (bi, i, off["gqa_q"] // gw)),
                  pl.BlockSpec((1, tm, kw), lambda bi, i: (bi, i, off["gqa_k"] // kw)),
                  pl.BlockSpec((1, tm, kw), lambda bi, i: (bi, i, off["gqa_v"] // kw)),
                  tab, tab, tab,
                  pl.BlockSpec((2, HEAD_DIM), lambda bi, i: (0, 0))],
        out_specs=[pl.BlockSpec((1, tm, gw), lambda bi, i: (bi, i, 0)),
                   pl.BlockSpec((1, tm, kw), lambda bi, i: (bi, i, 0)),
                   pl.BlockSpec((1, tm, kw), lambda bi, i: (bi, i, 0))],
        out_shape=[jax.ShapeDtypeStruct((b, t, gw), BF16),
                   jax.ShapeDtypeStruct((b, t, kw), BF16),
                   jax.ShapeDtypeStruct((b, t, kw), BF16)],
        compiler_params=_params(("parallel", "parallel")),
    )(u, u, u, *tables, qk_norm)


def _mla_prep_kernel(x_ref, kpe_ref, c_ref, su_ref, sd_ref, qn_ref, kvn_ref, wq_ref, wk_ref, wv_ref,
                     qo_ref, ko_ref, vo_ref, *, g):
    c, su, sd = c_ref[...], su_ref[...], sd_ref[...]
    scale = (MLA_NOPE + MLA_ROPE) ** -0.5
    x = x_ref[0]
    cq = _rms(x[:, :MLA_Q_RANK], qn_ref[...])
    ckv = _rms(x[:, MLA_Q_RANK:], kvn_ref[...])
    q = _dot(cq, wq_ref[...])
    kn = _dot(ckv, wk_ref[...])
    vo_ref[0] = _dot(ckv, wv_ref[...]).astype(BF16)
    kpe = _rope(kpe_ref[0], c, su, sd, 16).astype(BF16)
    for h in range(g):
        lo = h * MLA_QK_PAD
        qo_ref[0, :, lo:lo + LANE] = (q[:, lo:lo + LANE] * scale).astype(BF16)
        qo_ref[0, :, lo + LANE:lo + 2 * LANE] = (
            _rope(q[:, lo + LANE:lo + 2 * LANE], c, su, sd, 16) * scale).astype(BF16)
        ko_ref[0, :, lo:lo + LANE] = kn[:, h * LANE:(h + 1) * LANE].astype(BF16)
        ko_ref[0, :, lo + LANE:lo + 2 * LANE] = kpe


def _mla_prep(u, tables, q_norm, kv_norm, wq, wk, wv, g, off):
    b, t, _ = u.shape
    tm = _row_tile(t)
    cw = MLA_Q_RANK + MLA_KV_RANK
    tab = pl.BlockSpec((tm, LANE), lambda bi, i: (i, 0))

    def full(a):
        return pl.BlockSpec(a.shape, lambda bi, i: (0,) * a.ndim)

    return pl.pallas_call(
        functools.partial(_mla_prep_kernel, g=g),
        name="mla_prep",
        grid=(b, t // tm),
        in_specs=[pl.BlockSpec((1, tm, cw), lambda bi, i: (bi, i, off["mla_cqkv"] // cw)),
                  pl.BlockSpec((1, tm, LANE), lambda bi, i: (bi, i, off["mla_kpe"] // LANE)),
                  tab, tab, tab, full(q_norm), full(kv_norm), full(wq), full(wk), full(wv)],
        out_specs=[pl.BlockSpec((1, tm, g * MLA_QK_PAD), lambda bi, i: (bi, i, 0)),
                   pl.BlockSpec((1, tm, g * MLA_QK_PAD), lambda bi, i: (bi, i, 0)),
                   pl.BlockSpec((1, tm, g * MLA_DV), lambda bi, i: (bi, i, 0))],
        out_shape=[jax.ShapeDtypeStruct((b, t, g * MLA_QK_PAD), BF16),
                   jax.ShapeDtypeStruct((b, t, g * MLA_QK_PAD), BF16),
                   jax.ShapeDtypeStruct((b, t, g * MLA_DV), BF16)],
        compiler_params=_params(("parallel", "parallel")),
    )(u, u, *tables, q_norm, kv_norm, wq, wk, wv)


HEADS_PER_STEP = 2


def _attn_kernel(q_ref, k_ref, v_ref, o_ref, *, ctx_len, tq, rep, dq, dv):
    heads = range(HEADS_PER_STEP)

    def attend(nk):
        s = [lax.dot_general(q_ref[0, :, j * dq:(j + 1) * dq], k_ref[0, :nk, (j // rep) * dq:(j // rep + 1) * dq],
                             (((1,), (1,)), ((), ())), preferred_element_type=F32) for j in heads]
        p = [jnp.exp(s[j] - jnp.max(s[j], axis=-1, keepdims=True)) for j in heads]
        l = [jnp.sum(p[j], axis=-1, keepdims=True) for j in heads]
        o = [jnp.dot(p[j].astype(BF16), v_ref[0, :nk, (j // rep) * dv:(j // rep + 1) * dv],
                     preferred_element_type=F32) for j in heads]
        for j in heads:
            o_ref[0, :, j * dv:(j + 1) * dv] = (o[j] / l[j]).astype(o_ref.dtype)

    is_ctx = pl.program_id(2) * tq < ctx_len

    @pl.when(is_ctx)
    def _():
        attend(ctx_len)

    @pl.when(jnp.logical_not(is_ctx))
    def _():
        attend(k_ref.shape[1])


def _attention(q, k, v, heads, kv_heads, ctx_len):
    b, t, _ = q.shape
    dq, dv = q.shape[2] // heads, v.shape[2] // kv_heads
    rep = heads // kv_heads
    hp = HEADS_PER_STEP
    assert heads % hp == 0 and (hp % rep == 0 or rep % hp == 0)
    kvp = max(hp // rep, 1)
    tq = math.gcd(256, ctx_len)
    return pl.pallas_call(
        functools.partial(_attn_kernel, ctx_len=ctx_len, tq=tq, rep=rep, dq=dq, dv=dv),
        name="attn",
        grid=(b, heads // hp, t // tq),
        in_specs=[pl.BlockSpec((1, tq, hp * dq), lambda bi, h, i: (bi, i, h)),
                  pl.BlockSpec((1, t, kvp * dq), lambda bi, h, i: (bi, 0, h * hp // (rep * kvp))),
                  pl.BlockSpec((1, t, kvp * dv), lambda bi, h, i: (bi, 0, h * hp // (rep * kvp)))],
        out_specs=pl.BlockSpec((1, tq, hp * dv), lambda bi, h, i: (bi, i, h)),
        out_shape=jax.ShapeDtypeStruct((b, t, heads * dv), BF16),
        compiler_params=_params(("parallel", "parallel", "parallel"), VMEM_LIMIT),
    )(q, k, v)


def _head_norm(x, w_row, g):
    parts = []
    for h in range(g):
        xs = x[:, h * HEAD_DIM:(h + 1) * HEAD_DIM]
        parts.append(xs * lax.rsqrt(jnp.mean(xs * xs, axis=-1, keepdims=True) + NORM_EPS))
    return jnp.concatenate(parts, axis=-1) * w_row


def _outproj_kernel(af_ref, ab_ref, ga_ref, yb_ref, cf_ref, cb_ref, gc_ref, yd_ref, h_ref, mod_ref, g_ref,
                    na_ref, nc_ref, w_ref, o_ref, *, g, ctx_len, tm):
    row = pl.program_id(1) * tm + lax.broadcasted_iota(jnp.int32, (tm, 1), 0)
    ga = ga_ref[0]
    ya = _head_norm(af_ref[0] + ab_ref[0], na_ref[...], g) * (ga * _sigmoid(ga))
    yc = _head_norm(cf_ref[0] + cb_ref[0], nc_ref[...], g) * _sigmoid(gc_ref[0])
    y = jnp.concatenate([ya.astype(BF16), yb_ref[0], yc.astype(BF16), yd_ref[0]], axis=-1)
    z = jnp.dot(y, w_ref[...], preferred_element_type=F32)
    o_ref[0] = h_ref[0] + _mod_rows(mod_ref, 2, row, ctx_len) * _rms(z, g_ref[1:2, :])


def _outproj(oa, ob, oc, od, u, h, mod, g4, na_row, nc_row, w, g, ctx_len, off):
    b, t, d = h.shape
    tm = _row_tile(t)
    gw = g * HEAD_DIM

    def rows(width, blk=0):
        return pl.BlockSpec((1, tm, width), lambda bi, i: (bi, i, blk))

    return pl.pallas_call(
        functools.partial(_outproj_kernel, g=g, ctx_len=ctx_len, tm=tm),
        name="outproj",
        grid=(b, t // tm),
        in_specs=[rows(gw), rows(gw), rows(gw, off["gdn_gate"] // gw), rows(gw),
                  rows(gw), rows(gw), rows(gw, off["ml_og"] // gw), rows(gw),
                  rows(d),
                  pl.BlockSpec((1, 2, 6, d), lambda bi, i: (bi, 0, 0, 0)),
                  pl.BlockSpec((4, d), lambda bi, i: (0, 0)),
                  pl.BlockSpec((1, gw), lambda bi, i: (0, 0)),
                  pl.BlockSpec((1, gw), lambda bi, i: (0, 0)),
                  pl.BlockSpec(w.shape, lambda bi, i: (0, 0))],
        out_specs=rows(d),
        out_shape=jax.ShapeDtypeStruct((b, t, d), F32),
        compiler_params=_params(("parallel", "parallel"), VMEM_LIMIT),
    )(oa[0], oa[1], u, ob, oc[0], oc[1], u, od, h, mod, g4, na_row, nc_row, w)


def _ffn_kernel(h_ref, mod_ref, g_ref, w1_ref, w2_ref, o_ref, xn_ref, acc_ref, *, ctx_len, tm):
    k = pl.program_id(2)
    row = pl.program_id(1) * tm + lax.broadcasted_iota(jnp.int32, (tm, 1), 0)

    @pl.when(k == 0)
    def _():
        y = _rms(h_ref[0], g_ref[2:3, :])
        y = y * (1.0 + _mod_rows(mod_ref, 4, row, ctx_len)) + _mod_rows(mod_ref, 3, row, ctx_len)
        xn_ref[...] = y.astype(BF16)
        acc_ref[...] = jnp.zeros_like(acc_ref)

    hid = jnp.maximum(jnp.dot(xn_ref[...], w1_ref[...], preferred_element_type=F32), 0.0)
    acc_ref[...] += jnp.dot((hid * hid).astype(BF16), w2_ref[...], preferred_element_type=F32)

    @pl.when(k == pl.num_programs(2) - 1)
    def _():
        o_ref[0] = h_ref[0] + _mod_rows(mod_ref, 5, row, ctx_len) * _rms(acc_ref[...], g_ref[3:4, :])


def _ffn(h, mod, g4, w1, w2, ctx_len):
    b, t, d = h.shape
    ff = w1.shape[1]
    tm = _row_tile(t)
    tf = _pick_tile(ff, (1024, 512, 256, 128))
    return pl.pallas_call(
        functools.partial(_ffn_kernel, ctx_len=ctx_len, tm=tm),
        name="ffn",
        grid=(b, t // tm, ff // tf),
        in_specs=[pl.BlockSpec((1, tm, d), lambda bi, i, k: (bi, i, 0)),
                  pl.BlockSpec((1, 2, 6, d), lambda bi, i, k: (bi, 0, 0, 0)),
                  pl.BlockSpec((4, d), lambda bi, i, k: (0, 0)),
                  pl.BlockSpec((d, tf), lambda bi, i, k: (0, k)),
                  pl.BlockSpec((tf, d), lambda bi, i, k: (k, 0))],
        out_specs=pl.BlockSpec((1, tm, d), lambda bi, i, k: (bi, i, 0)),
        out_shape=jax.ShapeDtypeStruct((b, t, d), F32),
        scratch_shapes=[pltpu.VMEM((tm, d), BF16), pltpu.VMEM((tm, d), F32)],
        compiler_params=_params(("parallel", "parallel", "arbitrary"), VMEM_LIMIT),
    )(h, mod, g4, w1, w2)


def _rope_tables(seq, ctx_len, dim):
    quarter = dim // 4
    t = jnp.arange(seq, dtype=jnp.int32)
    inv_freq = ROPE_BASE ** (-jnp.arange(quarter, dtype=F32) / quarter)
    ang_r = (t // GRID_W).astype(F32)[:, None] * inv_freq
    ang_c = (t % GRID_W).astype(F32)[:, None] * inv_freq
    zero = jnp.zeros_like(ang_r)
    cos = jnp.concatenate([jnp.cos(ang_r)] * 2 + [jnp.cos(ang_c)] * 2, axis=-1)
    s_up = jnp.concatenate([-jnp.sin(ang_r), zero, -jnp.sin(ang_c), zero], axis=-1)
    s_dn = jnp.concatenate([zero, jnp.sin(ang_r), zero, jnp.sin(ang_c)], axis=-1)

    def finish(tab, fill):
        tab = jnp.pad(tab, ((0, 0), (0, LANE - dim)), constant_values=fill)
        return jnp.pad(tab, ((ctx_len, 0), (0, 0)), constant_values=fill)

    return finish(cos, 1.0), finish(s_up, 0.0), finish(s_dn, 0.0)


def _lane_row(v):
    v = v.reshape(1, -1)
    return jnp.pad(v, ((0, 0), (0, LANE - v.shape[1])))


def kernel(x, c, ctx, c_ctx, w_mod, b_mod, g_norm, w_in, gdn_conv, gdn_a_log, gdn_dt_bias, gdn_norm,
           gqa_qk_norm, mlstm_gate_bias, mlstm_norm, mla_q_norm, mla_kv_norm, mla_w_qb, mla_w_kvb,
           w_out, w_ff1, w_ff2):
    bsz, seq, d = x.shape
    ctx_len = ctx.shape[1]
    depth = w_in.shape[0]
    g = w_out.shape[1] // (4 * HEAD_DIM)
    assert ctx_len % CHUNK == 0 and seq % CHUNK == 0 and seq % GRID_W == 0
    off, _ = _layout(g)
    cols, n_pad, n_src = _packed_columns(g)
    assert n_src == w_in.shape[2]

    cols = jnp.asarray(cols, jnp.int32)
    w_in_p = jnp.where(cols >= 0, jnp.take(w_in, jnp.maximum(cols, 0), axis=2), 0.0).astype(BF16)
    wq = mla_w_qb.reshape(depth, MLA_Q_RANK, g, MLA_NOPE + MLA_ROPE)
    wq = jnp.pad(wq, ((0, 0), (0, 0), (0, 0), (0, MLA_QK_PAD - MLA_NOPE - MLA_ROPE)))
    wq = wq.reshape(depth, MLA_Q_RANK, g * MLA_QK_PAD).astype(BF16)
    wkv = mla_w_kvb.reshape(depth, MLA_KV_RANK, g, MLA_NOPE + MLA_DV)
    wk = wkv[..., :MLA_NOPE].reshape(depth, MLA_KV_RANK, g * MLA_NOPE).astype(BF16)
    wv = wkv[..., MLA_NOPE:].reshape(depth, MLA_KV_RANK, g * MLA_DV).astype(BF16)
    w_out_b, w_ff1_b, w_ff2_b = w_out.astype(BF16), w_ff1.astype(BF16), w_ff2.astype(BF16)
    gqa_tab = _rope_tables(seq, ctx_len, HEAD_DIM)
    mla_tab = _rope_tables(seq, ctx_len, MLA_ROPE)

    n_rows = -(-(bsz + 1) // 8) * 8
    c_rows = jnp.concatenate([c, c_ctx[None, :], jnp.zeros((n_rows - bsz - 1, d), F32)], axis=0)
    mod_all = _mod_all(c_rows, w_mod, b_mod).reshape(depth, n_rows, 6, d)
    mod_all = jnp.stack([jnp.broadcast_to(mod_all[:, bsz:bsz + 1], (depth, bsz, 6, d)), mod_all[:, :bsz]], axis=2)

    h = jnp.concatenate([ctx, x], axis=1)
    for l in range(depth):
        mod, g4 = mod_all[l], g_norm[l]
        u = _inproj(h, mod, g4, w_in_p[l], ctx_len)
        qkv = _gdn_prep(u, gdn_conv[l], g, ctx_len, off)
        oa = _gdn_scan(qkv, u, _lane_row(gdn_a_log[l]), _lane_row(gdn_dt_bias[l]), g, ctx_len, off)
        qn, kn, vn = _gqa_prep(u, gqa_tab, gqa_qk_norm[l], g, off)
        ob = _attention(qn, kn, vn, g, g // 2, ctx_len)
        oc = _mlstm_scan(u, _lane_row(mlstm_gate_bias[l]), g, ctx_len, off)
        qm, km, vm = _mla_prep(u, mla_tab, mla_q_norm[l].reshape(1, -1), mla_kv_norm[l].reshape(1, -1),
                               wq[l], wk[l], wv[l], g, off)
        od = _attention(qm, km, vm, g, g, ctx_len)
        h = _outproj(oa, ob, oc, od, u, h, mod, g4, jnp.tile(gdn_norm[l], g).reshape(1, -1),
                     mlstm_norm[l].reshape(1, -1), w_out_b[l], g, ctx_len, off)
        h = _ffn(h, mod, g4, w_ff1_b[l], w_ff2_b[l], ctx_len)
    return h[:, ctx_len:, :]
```

```python
import functools
import math

import jax
import jax.numpy as jnp
from jax import lax
from jax.experimental import pallas as pl
from jax.experimental.pallas import tpu as pltpu

F32 = jnp.float32
BF16 = jnp.bfloat16

LANE = 128
HEAD_DIM = 128
CHUNK = 64
GRID_W = 64
ROPE_BASE = 10000.0
NORM_EPS = 1e-6
ML_DQK = 64
MLA_Q_RANK = 384
MLA_KV_RANK = 128
MLA_NOPE = 128
MLA_ROPE = 64
MLA_DV = 128
MLA_QK_PAD = 256
VMEM_LIMIT = 56 * 1024 * 1024
LOG2_E = math.log2(math.e)


def _dot(a, b):
    return jnp.dot(a.astype(BF16), b.astype(BF16), preferred_element_type=F32)


def _dot_nt(a, b):
    return lax.dot_general(a.astype(BF16), b.astype(BF16), (((1,), (1,)), ((), ())),
                           preferred_element_type=F32)


def _dot_tn(a, b):
    return lax.dot_general(a.astype(BF16), b.astype(BF16), (((0,), (0,)), ((), ())),
                           preferred_element_type=F32)


def _sigmoid(x):
    return 1.0 / (1.0 + jnp.exp(-x))


def _softplus(x):
    return jnp.maximum(x, 0.0) + jnp.log(1.0 + jnp.exp(-jnp.abs(x)))


def _rms(x, w):
    return x * lax.rsqrt(jnp.mean(x * x, axis=-1, keepdims=True) + NORM_EPS) * w


def _params(sem, vmem=None):
    return pltpu.CompilerParams(dimension_semantics=sem, vmem_limit_bytes=vmem)


def _layout(g):
    gw = g * HEAD_DIM
    order = [("mla_cqkv", MLA_Q_RANK + MLA_KV_RANK),
             ("gdn_q", gw), ("gdn_k", gw), ("gdn_v", gw), ("gdn_gate", gw),
             ("ml_q", gw), ("ml_k", gw), ("ml_v", gw), ("ml_og", gw), ("gqa_q", gw),
             ("gqa_k", gw // 2), ("gqa_v", gw // 2),
             ("gdn_ab", LANE), ("ml_if", LANE), ("mla_kpe", LANE)]
    off, pos = {}, 0
    for name, width in order:
        assert pos % width == 0, (name, pos, width)
        off[name] = pos
        pos += width
    return off, pos


def _packed_columns(g):
    gw = g * HEAD_DIM
    off, total = _layout(g)
    n_pad = -(-total // 512) * 512
    idx = [-1] * n_pad

    def put(name, src0, n, dst0=0):
        for t in range(n):
            idx[off[name] + dst0 + t] = src0 + t

    p = 0
    put("gdn_q", p, gw); p += gw
    put("gdn_k", p, gw); p += gw
    put("gdn_v", p, gw); p += gw
    put("gdn_gate", p, gw); p += gw
    put("gdn_ab", p, 4 * g); p += 4 * g
    put("gqa_q", p, gw); p += gw
    put("gqa_k", p, gw // 2); p += gw // 2
    put("gqa_v", p, gw // 2); p += gw // 2
    for name in ("ml_q", "ml_k"):
        for h in range(g):
            put(name, p + h * ML_DQK, ML_DQK, h * HEAD_DIM)
        p += g * ML_DQK
    put("ml_v", p, gw); p += gw
    put("ml_og", p, gw); p += gw
    put("ml_if", p, 4 * g); p += 4 * g
    put("mla_cqkv", p, MLA_Q_RANK + MLA_KV_RANK); p += MLA_Q_RANK + MLA_KV_RANK
    put("mla_kpe", p, MLA_ROPE); p += MLA_ROPE
    return idx, n_pad, p


def _pick_tile(n, candidates):
    for c in candidates:
        if n % c == 0:
            return c
    raise ValueError(f"no tile for {n} in {candidates}")


def _row_tile(t, limit=640):
    for c in range(limit - limit % 16, 15, -16):
        if t % c == 0:
            return c
    raise ValueError(f"no row tile for {t}")


def _mod_kernel(c_ref, w_ref, b_ref, o_ref):
    c = c_ref[...]
    o_ref[0] = _dot(c * _sigmoid(c), w_ref[0]) + b_ref[0]


def _mod_all(c_rows, w_mod, b_mod):
    depth, d, n = w_mod.shape
    r = c_rows.shape[0]
    tn = _pick_tile(n, (1024, 512, 256, 128))
    return pl.pallas_call(
        _mod_kernel,
        name="mod",
        grid=(depth, n // tn),
        in_specs=[pl.BlockSpec((r, d), lambda l, j: (0, 0)),
                  pl.BlockSpec((1, d, tn), lambda l, j: (l, 0, j)),
                  pl.BlockSpec((1, 1, tn), lambda l, j: (l, 0, j))],
        out_specs=pl.BlockSpec((1, r, tn), lambda l, j: (l, 0, j)),
        out_shape=jax.ShapeDtypeStruct((depth, r, n), F32),
        compiler_params=_params(("parallel", "parallel"), VMEM_LIMIT),
    )(c_rows, w_mod, b_mod.reshape(depth, 1, n))


def _mod_rows(mod_ref, idx, row, ctx_len):
    if row is None:
        return mod_ref[0, 1, idx:idx + 1, :]
    return jnp.where(row < ctx_len, mod_ref[0, 0, idx:idx + 1, :], mod_ref[0, 1, idx:idx + 1, :])


def _unit_rms(x):
    return x * lax.rsqrt(jnp.mean(x * x, axis=-1, keepdims=True) + NORM_EPS)


def _per_tile_kind(tm, ctx_len, body, also=True):
    first = pl.program_id(1) * tm

    @pl.when(jnp.logical_and(also, first >= ctx_len))
    def _():
        body(None)

    @pl.when(jnp.logical_and(also, first < ctx_len))
    def _():
        body(first + lax.broadcasted_iota(jnp.int32, (tm, 1), 0))


def _prenorm_kernel(h_ref, mod_ref, g_ref, o_ref, *, ctx_len, tm):
    def body(row):
        gain = g_ref[0:1, :] * (1.0 + _mod_rows(mod_ref, 1, row, ctx_len))
        o_ref[0] = (_unit_rms(h_ref[0]) * gain + _mod_rows(mod_ref, 0, row, ctx_len)).astype(BF16)

    _per_tile_kind(tm, ctx_len, body)


def _inproj_kernel(x_ref, w_ref, o_ref):
    o_ref[0] = jnp.dot(x_ref[0], w_ref[...], preferred_element_type=F32)


def _inproj(h, mod, g4, w, ctx_len):
    b, t, d = h.shape
    n = w.shape[1]
    tm = _row_tile(t)
    xn = pl.pallas_call(
        functools.partial(_prenorm_kernel, ctx_len=ctx_len, tm=tm),
        name="prenorm",
        grid=(b, t // tm),
        in_specs=[pl.BlockSpec((1, tm, d), lambda bi, i: (bi, i, 0)),
                  pl.BlockSpec((1, 2, 6, d), lambda bi, i: (bi, 0, 0, 0)),
                  pl.BlockSpec((4, d), lambda bi, i: (0, 0))],
        out_specs=pl.BlockSpec((1, tm, d), lambda bi, i: (bi, i, 0)),
        out_shape=jax.ShapeDtypeStruct((b, t, d), BF16),
        compiler_params=_params(("parallel", "parallel"), VMEM_LIMIT),
    )(h, mod, g4)
    tm = _row_tile(t, 1280)
    tn = _pick_tile(n, (1536, 1024, 512))
    return pl.pallas_call(
        _inproj_kernel,
        name="inproj",
        grid=(n // tn, b, t // tm),
        in_specs=[pl.BlockSpec((1, tm, d), lambda j, bi, i: (bi, i, 0)),
                  pl.BlockSpec((d, tn), lambda j, bi, i: (0, j))],
        out_specs=pl.BlockSpec((1, tm, tn), lambda j, bi, i: (bi, i, j)),
        out_shape=jax.ShapeDtypeStruct((b, t, n), F32),
        compiler_params=_params(("parallel", "parallel", "parallel"), VMEM_LIMIT),
    )(xn, w)


def _gdn_prep_kernel(u_ref, w_ref, o_ref, *, g, ctx_len):
    j = pl.program_id(1)
    x = u_ref[0]
    t = x.shape[0]
    row = lax.broadcasted_iota(jnp.int32, (t, 1), 0)
    first = (row == 0) | (row == ctx_len)
    last = (row == ctx_len - 1) | (row == t - 1)
    x_prev = jnp.where(first, 0.0, pltpu.roll(x, 1, 0))
    x_next = jnp.where(last, 0.0, pltpu.roll(x, t - 1, 0))
    y = x_prev * w_ref[0:1, :] + x * w_ref[1:2, :] + x_next * w_ref[2:3, :]
    y = y * _sigmoid(y)
    yn = y * lax.rsqrt(jnp.sum(y * y, axis=-1, keepdims=True) + NORM_EPS)
    scale = jnp.where(j < g, HEAD_DIM ** -0.5, 1.0)
    o_ref[0] = jnp.where(j < 2 * g, yn * scale, y)


def _gdn_prep(u, conv_w, g, ctx_len, off):
    b, t, _ = u.shape
    blk0 = off["gdn_q"] // LANE
    return pl.pallas_call(
        functools.partial(_gdn_prep_kernel, g=g, ctx_len=ctx_len),
        name="gdn_prep",
        grid=(b, 3 * g),
        in_specs=[pl.BlockSpec((1, t, LANE), lambda bi, j: (bi, 0, blk0 + j)),
                  pl.BlockSpec((3, LANE), lambda bi, j: (0, j))],
        out_specs=pl.BlockSpec((1, t, LANE), lambda bi, j: (bi, 0, j)),
        out_shape=jax.ShapeDtypeStruct((b, t, 3 * g * LANE), F32),
        compiler_params=_params(("parallel", "parallel"), VMEM_LIMIT),
    )(u, conv_w)


def _chunk_masks(fwd):
    ri = lax.broadcasted_iota(jnp.int32, (CHUNK, CHUNK), 0)
    ci = lax.broadcasted_iota(jnp.int32, (CHUNK, CHUNK), 1)
    if fwd:
        return ri >= ci, ri > ci, ri == ci
    return ri <= ci, ri < ci, ri == ci


def _bcast(col):
    return jnp.broadcast_to(col, (CHUNK, LANE))


def _split3(x):
    x1 = x.astype(BF16)
    r = x - x1.astype(F32)
    x2 = r.astype(BF16)
    return x1, x2, (r - x2.astype(F32)).astype(BF16)


def _chunk_cumsum(x, incl):
    lmat = jnp.where(incl, 1.0, 0.0).astype(BF16)
    p1, p2, p3 = _split3(x)
    return (jnp.dot(lmat, p1, preferred_element_type=F32) + jnp.dot(lmat, p2, preferred_element_type=F32)
            + jnp.dot(lmat, p3, preferred_element_type=F32))


def _pair_diff(g_b):
    return g_b[:, :CHUNK] - g_b.T[:CHUNK, :]


def _tri_inverse_all(a_list, diag):
    ri = lax.broadcasted_iota(jnp.int32, (CHUNK, CHUNK), 0)
    ci = lax.broadcasted_iota(jnp.int32, (CHUNK, CHUNK), 1)
    eye = diag.astype(F32)
    ts = [jnp.where((ri >> 1) == (ci >> 1), eye - a, 0.0) for a in a_list]
    for lb in range(1, int(math.log2(CHUNK))):
        join = ((ri >> (lb + 1)) == (ci >> (lb + 1))) & ((ri >> lb) != (ci >> lb))
        us = [_dot(jnp.where(join, a, 0.0), t) for a, t in zip(a_list, ts)]
        ts = [t - _dot(t, u) for t, u in zip(ts, us)]
    return ts


def _gdn_scan_kernel(qf_ref, gf_ref, qb_ref, gb_ref, al_ref, dt_ref, of_ref, ob_ref, s_ref, *, g, nb):
    @pl.when(pl.program_id(1) == 0)
    def _():
        s_ref[...] = jnp.zeros_like(s_ref)

    gw = g * HEAD_DIM
    streams = range(nb * 2 * g)
    fwd, incl, strict, q, k, v, g_b, beta_b, dest = [], [], [], [], [], [], [], [], []
    diag = _chunk_masks(True)[2]
    for bb in range(nb):
        for d, (q_ref, gt_ref) in enumerate(((qf_ref, gf_ref), (qb_ref, gb_ref))):
            m_incl, m_strict, _ = _chunk_masks(d == 0)
            gates = gt_ref[bb]
            la = -jnp.exp(al_ref[...]) * _softplus(gates + dt_ref[...])
            g_all = _chunk_cumsum(la, m_incl)
            be = _sigmoid(gates)
            for h in range(g):
                c = d * g + h
                fwd.append(d == 0)
                incl.append(m_incl)
                strict.append(m_strict)
                dest.append((of_ref if d == 0 else ob_ref, bb, h))
                q.append(q_ref[bb, :, h * HEAD_DIM:(h + 1) * HEAD_DIM])
                k.append(q_ref[bb, :, gw + h * HEAD_DIM:gw + (h + 1) * HEAD_DIM])
                v.append(q_ref[bb, :, 2 * gw + h * HEAD_DIM:2 * gw + (h + 1) * HEAD_DIM])
                g_b.append(_bcast(g_all[:, c:c + 1]))
                beta_b.append(_bcast(be[:, 2 * g + c:2 * g + c + 1]))

    decay = [jnp.exp(jnp.where(incl[i], _pair_diff(g_b[i]), -jnp.inf)) for i in streams]
    kb = [k[i] * beta_b[i] for i in streams]
    kq = [_dot_nt(jnp.concatenate([kb[i], q[i]], axis=0), k[i]) for i in streams]
    a = [jnp.where(strict[i], kq[i][:CHUNK] * decay[i], 0.0) for i in streams]
    attn = [kq[i][CHUNK:] * decay[i] for i in streams]
    t = _tri_inverse_all(a, diag)
    eg = [jnp.exp(g_b[i]) for i in streams]
    vw = [_dot(t[i], jnp.concatenate([v[i] * beta_b[i], kb[i] * eg[i]], axis=1)) for i in streams]
    s = [s_ref[i] for i in streams]
    ws = [_dot(jnp.concatenate([vw[i][:, HEAD_DIM:], q[i] * eg[i]], axis=0), s[i]) for i in streams]
    v_new = [vw[i][:, :HEAD_DIM] - ws[i][:CHUNK] for i in streams]
    o = [ws[i][CHUNK:] + _dot(attn[i], v_new[i]) for i in streams]
    g_end = [g_b[i][CHUNK - 1:CHUNK, :] if fwd[i] else g_b[i][0:1, :] for i in streams]
    s_new = [s[i] * jnp.exp(g_end[i]) + _dot_tn(k[i] * jnp.exp(g_end[i] - g_b[i]), v_new[i]) for i in streams]
    for i in streams:
        o_ref, bb, h = dest[i]
        o_ref[bb, :, h * HEAD_DIM:(h + 1) * HEAD_DIM] = o[i]
        s_ref[i] = s_new[i]


SCAN_BATCH = 4


def _scan_batch(b):
    return math.gcd(b, SCAN_BATCH)


def _scan_chunk_maps(n_ctx, n_all):
    def fwd(bi, s):
        return (bi, s, 0)

    def bwd(bi, s):
        return (bi, jnp.where(s < n_ctx, n_ctx - 1 - s, n_all + n_ctx - 1 - s), 0)

    return fwd, bwd


def _gdn_scan(qkv, u, a_log_row, dt_row, g, ctx_len, off):
    b, t, _ = qkv.shape
    n_all, n_ctx = t // CHUNK, ctx_len // CHUNK
    fwd, bwd = _scan_chunk_maps(n_ctx, n_all)
    gblk = off["gdn_ab"] // LANE

    def gate_map(m):
        return lambda bi, s: (m(bi, s)[0], m(bi, s)[1], gblk)

    gw = g * HEAD_DIM
    nb = _scan_batch(b)
    return pl.pallas_call(
        functools.partial(_gdn_scan_kernel, g=g, nb=nb),
        name="gdn_scan",
        grid=(b // nb, n_all),
        in_specs=[pl.BlockSpec((nb, CHUNK, 3 * gw), fwd),
                  pl.BlockSpec((nb, CHUNK, LANE), gate_map(fwd)),
                  pl.BlockSpec((nb, CHUNK, 3 * gw), bwd),
                  pl.BlockSpec((nb, CHUNK, LANE), gate_map(bwd)),
                  pl.BlockSpec((1, LANE), lambda bi, s: (0, 0)),
                  pl.BlockSpec((1, LANE), lambda bi, s: (0, 0))],
        out_specs=[pl.BlockSpec((nb, CHUNK, gw), fwd), pl.BlockSpec((nb, CHUNK, gw), bwd)],
        out_shape=[jax.ShapeDtypeStruct((b, t, gw), F32)] * 2,
        scratch_shapes=[pltpu.VMEM((nb * 2 * g, HEAD_DIM, HEAD_DIM), F32)],
        compiler_params=_params(("parallel", "arbitrary")),
    )(qkv, u, qkv, u, a_log_row, dt_row)


def _mlstm_scan_kernel(qf_ref, kf_ref, vf_ref, gf_ref, qb_ref, kb_ref, vb_ref, gb_ref, bias_ref,
                       of_ref, ob_ref, c_ref, n_ref, m_ref, *, g, nb):
    @pl.when(pl.program_id(1) == 0)
    def _():
        c_ref[...] = jnp.zeros_like(c_ref)
        n_ref[...] = jnp.zeros_like(n_ref)
        m_ref[...] = jnp.zeros_like(m_ref)

    streams = range(nb * 2 * g)
    fwd, incl, q, k, v, b_b, ip_b, dest = [], [], [], [], [], [], [], []
    dirs = ((qf_ref, kf_ref, vf_ref, gf_ref), (qb_ref, kb_ref, vb_ref, gb_ref))
    for bb in range(nb):
        for d, (q_ref, k_ref, v_ref, gt_ref) in enumerate(dirs):
            m_incl = _chunk_masks(d == 0)[0]
            gates = gt_ref[bb] + bias_ref[...]
            logsig = jnp.minimum(gates, 0.0) - jnp.log(1.0 + jnp.exp(-jnp.abs(gates)))
            b_all = _chunk_cumsum(logsig, m_incl)
            for h in range(g):
                c = d * g + h
                sl = slice(h * HEAD_DIM, (h + 1) * HEAD_DIM)
                fwd.append(d == 0)
                incl.append(m_incl)
                dest.append((of_ref if d == 0 else ob_ref, bb, h))
                q.append(q_ref[bb, :, sl] * (ML_DQK ** -0.5))
                k.append(k_ref[bb, :, sl])
                v.append(v_ref[bb, :, sl])
                b_b.append(_bcast(b_all[:, 2 * g + c:2 * g + c + 1]))
                ip_b.append(_bcast(gates[:, c:c + 1]))

    dmat = [jnp.where(incl[i], _pair_diff(b_b[i]) + ip_b[i].T[:CHUNK, :], -jnp.inf) for i in streams]
    qk = [_dot_nt(q[i], k[i]) for i in streams]
    cm = [c_ref[i] for i in streams]
    nrow = [n_ref[i] for i in streams]
    mrow = [m_ref[i] for i in streams]
    b_end = [b_b[i][CHUNK - 1:CHUNK, :] if fwd[i] else b_b[i][0:1, :] for i in streams]
    w_b = [b_end[i] - b_b[i] + ip_b[i] for i in streams]
    inter = [b_b[i] + mrow[i] for i in streams]
    m_t = [jnp.maximum(inter[i], jnp.max(dmat[i], axis=1, keepdims=True)) for i in streams]
    s_inter = [jnp.exp(inter[i] - m_t[i]) for i in streams]
    p = [jnp.exp(dmat[i] - m_t[i][:, :CHUNK]) * qk[i] for i in streams]
    num = [s_inter[i] * _dot(q[i], cm[i]) + _dot(p[i], v[i]) for i in streams]
    den = [s_inter[i] * jnp.sum(q[i] * nrow[i], axis=1, keepdims=True) + jnp.sum(p[i], axis=1, keepdims=True)
           for i in streams]
    hh = [num[i] / jnp.maximum(jnp.abs(den[i]), jnp.exp(-m_t[i])) for i in streams]
    m_new = [jnp.maximum(b_end[i] + mrow[i], jnp.max(w_b[i], axis=0, keepdims=True)) for i in streams]
    s_old = [jnp.exp(b_end[i] + mrow[i] - m_new[i]) for i in streams]
    wk = [jnp.exp(w_b[i] - m_new[i]) * k[i] for i in streams]
    cm_new = [s_old[i] * cm[i] + _dot_tn(wk[i], v[i]) for i in streams]
    n_new = [s_old[i] * nrow[i] + jnp.sum(wk[i], axis=0, keepdims=True) for i in streams]
    for i in streams:
        o_ref, bb, h = dest[i]
        o_ref[bb, :, h * HEAD_DIM:(h + 1) * HEAD_DIM] = hh[i]
        c_ref[i] = cm_new[i]
        n_ref[i] = n_new[i]
        m_ref[i] = m_new[i]


def _mlstm_scan(u, bias_row, g, ctx_len, off):
    b, t, _ = u.shape
    n_all, n_ctx = t // CHUNK, ctx_len // CHUNK
    fwd, bwd = _scan_chunk_maps(n_ctx, n_all)
    gw = g * HEAD_DIM

    def col_map(m, blk):
        return lambda bi, s: (m(bi, s)[0], m(bi, s)[1], blk)

    nb = _scan_batch(b)

    def specs(m):
        return [pl.BlockSpec((nb, CHUNK, gw), col_map(m, off["ml_q"] // gw)),
                pl.BlockSpec((nb, CHUNK, gw), col_map(m, off["ml_k"] // gw)),
                pl.BlockSpec((nb, CHUNK, gw), col_map(m, off["ml_v"] // gw)),
                pl.BlockSpec((nb, CHUNK, LANE), col_map(m, off["ml_if"] // LANE))]

    return pl.pallas_call(
        functools.partial(_mlstm_scan_kernel, g=g, nb=nb),
        name="mlstm_scan",
        grid=(b // nb, n_all),
        in_specs=specs(fwd) + specs(bwd) + [pl.BlockSpec((1, LANE), lambda bi, s: (0, 0))],
        out_specs=[pl.BlockSpec((nb, CHUNK, gw), fwd), pl.BlockSpec((nb, CHUNK, gw), bwd)],
        out_shape=[jax.ShapeDtypeStruct((b, t, gw), F32)] * 2,
        scratch_shapes=[pltpu.VMEM((nb * 2 * g, HEAD_DIM, HEAD_DIM), F32),
                        pltpu.VMEM((nb * 2 * g, 1, HEAD_DIM), F32),
                        pltpu.VMEM((nb * 2 * g, 1, HEAD_DIM), F32)],
        compiler_params=_params(("parallel", "arbitrary")),
    )(u, u, u, u, u, u, u, u, bias_row)


def _rope(x, c, s_up, s_dn, shift):
    return x * c + pltpu.roll(x, LANE - shift, 1) * s_up + pltpu.roll(x, shift, 1) * s_dn


def _gqa_prep_kernel(q_ref, k_ref, v_ref, c_ref, su_ref, sd_ref, n_ref, qo_ref, ko_ref, vo_ref, *, g):
    c, su, sd = c_ref[...], su_ref[...], sd_ref[...]
    scale = HEAD_DIM ** -0.5 * LOG2_E
    for h in range(g):
        sl = slice(h * HEAD_DIM, (h + 1) * HEAD_DIM)
        qo_ref[0, :, sl] = (_rope(_rms(q_ref[0, :, sl], n_ref[0:1, :]), c, su, sd, 32) * scale).astype(BF16)
    for h in range(g // 2):
        sl = slice(h * HEAD_DIM, (h + 1) * HEAD_DIM)
        ko_ref[0, :, sl] = _rope(_rms(k_ref[0, :, sl], n_ref[1:2, :]), c, su, sd, 32).astype(BF16)
    vo_ref[0] = v_ref[0].astype(BF16)


def _gqa_prep(u, tables, qk_norm, g, off):
    b, t, _ = u.shape
    tm = _row_tile(t)
    gw, kw = g * HEAD_DIM, g * HEAD_DIM // 2
    tab = pl.BlockSpec((tm, LANE), lambda bi, i: (i, 0))
    return pl.pallas_call(
        functools.partial(_gqa_prep_kernel, g=g),
        name="gqa_prep",
        grid=(b, t // tm),
        in_specs=[pl.BlockSpec((1, tm, gw), lambda bi, i: (bi, i, off["gqa_q"] // gw)),
                  pl.BlockSpec((1, tm, kw), lambda bi, i: (bi, i, off["gqa_k"] // kw)),
                  pl.BlockSpec((1, tm, kw), lambda bi, i: (bi, i, off["gqa_v"] // kw)),
                  tab, tab, tab,
                  pl.BlockSpec((2, HEAD_DIM), lambda bi, i: (0, 0))],
        out_specs=[pl.BlockSpec((1, tm, gw), lambda bi, i: (bi, i, 0)),
                   pl.BlockSpec((1, tm, kw), lambda bi, i: (bi, i, 0)),
                   pl.BlockSpec((1, tm, kw), lambda bi, i: (bi, i, 0))],
        out_shape=[jax.ShapeDtypeStruct((b, t, gw), BF16),
                   jax.ShapeDtypeStruct((b, t, kw), BF16),
                   jax.ShapeDtypeStruct((b, t, kw), BF16)],
        compiler_params=_params(("parallel", "parallel")),
    )(u, u, u, *tables, qk_norm)


def _mla_prep_kernel(x_ref, kpe_ref, c_ref, su_ref, sd_ref, qn_ref, kvn_ref, wq_ref, wk_ref, wv_ref,
                     qo_ref, ko_ref, vo_ref, *, g):
    c, su, sd = c_ref[...], su_ref[...], sd_ref[...]
    scale = (MLA_NOPE + MLA_ROPE) ** -0.5 * LOG2_E
    x = x_ref[0]
    cq = _rms(x[:, :MLA_Q_RANK], qn_ref[...])
    ckv = _rms(x[:, MLA_Q_RANK:], kvn_ref[...])
    q = _dot(cq, wq_ref[...])
    kn = _dot(ckv, wk_ref[...])
    vo_ref[0] = _dot(ckv, wv_ref[...]).astype(BF16)
    kpe = _rope(kpe_ref[0], c, su, sd, 16).astype(BF16)
    for h in range(g):
        lo = h * MLA_QK_PAD
        qo_ref[0, :, lo:lo + LANE] = (q[:, lo:lo + LANE] * scale).astype(BF16)
        qo_ref[0, :, lo + LANE:lo + 2 * LANE] = (
            _rope(q[:, lo + LANE:lo + 2 * LANE], c, su, sd, 16) * scale).astype(BF16)
        ko_ref[0, :, lo:lo + LANE] = kn[:, h * LANE:(h + 1) * LANE].astype(BF16)
        ko_ref[0, :, lo + LANE:lo + 2 * LANE] = kpe


def _mla_prep(u, tables, q_norm, kv_norm, wq, wk, wv, g, off):
    b, t, _ = u.shape
    tm = _row_tile(t)
    cw = MLA_Q_RANK + MLA_KV_RANK
    tab = pl.BlockSpec((tm, LANE), lambda bi, i: (i, 0))

    def full(a):
        return pl.BlockSpec(a.shape, lambda bi, i: (0,) * a.ndim)

    return pl.pallas_call(
        functools.partial(_mla_prep_kernel, g=g),
        name="mla_prep",
        grid=(b, t // tm),
        in_specs=[pl.BlockSpec((1, tm, cw), lambda bi, i: (bi, i, off["mla_cqkv"] // cw)),
                  pl.BlockSpec((1, tm, LANE), lambda bi, i: (bi, i, off["mla_kpe"] // LANE)),
                  tab, tab, tab, full(q_norm), full(kv_norm), full(wq), full(wk), full(wv)],
        out_specs=[pl.BlockSpec((1, tm, g * MLA_QK_PAD), lambda bi, i: (bi, i, 0)),
                   pl.BlockSpec((1, tm, g * MLA_QK_PAD), lambda bi, i: (bi, i, 0)),
                   pl.BlockSpec((1, tm, g * MLA_DV), lambda bi, i: (bi, i, 0))],
        out_shape=[jax.ShapeDtypeStruct((b, t, g * MLA_QK_PAD), BF16),
                   jax.ShapeDtypeStruct((b, t, g * MLA_QK_PAD), BF16),
                   jax.ShapeDtypeStruct((b, t, g * MLA_DV), BF16)],
        compiler_params=_params(("parallel", "parallel")),
    )(u, u, *tables, q_norm, kv_norm, wq, wk, wv)


HEADS_PER_STEP = 2


def _attn_kernel(q_ref, k_ref, v_ref, o_ref, *, ctx_len, tq, rep, dq, dv):
    heads = range(HEADS_PER_STEP)

    def attend(nk):
        s = [lax.dot_general(q_ref[0, :, j * dq:(j + 1) * dq], k_ref[0, :nk, (j // rep) * dq:(j // rep + 1) * dq],
                             (((1,), (1,)), ((), ())), preferred_element_type=F32) for j in heads]
        p = [jnp.exp2(s[j] - jnp.max(s[j], axis=-1, keepdims=True)) for j in heads]
        l = [jnp.sum(p[j], axis=-1, keepdims=True) for j in heads]
        o = [jnp.dot(p[j].astype(BF16), v_ref[0, :nk, (j // rep) * dv:(j // rep + 1) * dv],
                     preferred_element_type=F32) for j in heads]
        for j in heads:
            o_ref[0, :, j * dv:(j + 1) * dv] = (o[j] / l[j]).astype(o_ref.dtype)

    is_ctx = pl.program_id(2) * tq < ctx_len

    @pl.when(is_ctx)
    def _():
        attend(ctx_len)

    @pl.when(jnp.logical_not(is_ctx))
    def _():
        attend(k_ref.shape[1])


def _attention(q, k, v, heads, kv_heads, ctx_len):
    b, t, _ = q.shape
    dq, dv = q.shape[2] // heads, v.shape[2] // kv_heads
    rep = heads // kv_heads
    hp = HEADS_PER_STEP
    assert heads % hp == 0 and (hp % rep == 0 or rep % hp == 0)
    kvp = max(hp // rep, 1)
    tq = math.gcd(256, ctx_len)
    return pl.pallas_call(
        functools.partial(_attn_kernel, ctx_len=ctx_len, tq=tq, rep=rep, dq=dq, dv=dv),
        name="attn",
        grid=(b, heads // hp, t // tq),
        in_specs=[pl.BlockSpec((1, tq, hp * dq), lambda bi, h, i: (bi, i, h)),
                  pl.BlockSpec((1, t, kvp * dq), lambda bi, h, i: (bi, 0, h * hp // (rep * kvp))),
                  pl.BlockSpec((1, t, kvp * dv), lambda bi, h, i: (bi, 0, h * hp // (rep * kvp)))],
        out_specs=pl.BlockSpec((1, tq, hp * dv), lambda bi, h, i: (bi, i, h)),
        out_shape=jax.ShapeDtypeStruct((b, t, heads * dv), BF16),
        compiler_params=_params(("parallel", "parallel", "parallel"), VMEM_LIMIT),
    )(q, k, v)


def _head_norm(x, w_row, g):
    parts = []
    for h in range(g):
        xs = x[:, h * HEAD_DIM:(h + 1) * HEAD_DIM]
        parts.append(xs * lax.rsqrt(jnp.mean(xs * xs, axis=-1, keepdims=True) + NORM_EPS))
    return jnp.concatenate(parts, axis=-1) * w_row


def _outproj_kernel(af_ref, ab_ref, ga_ref, yb_ref, cf_ref, cb_ref, gc_ref, yd_ref, h_ref, mod_ref, g_ref,
                    na_ref, nc_ref, w_ref, o_ref, *, g, ctx_len, tm):
    ga = ga_ref[0]
    ya = _head_norm(af_ref[0] + ab_ref[0], na_ref[...], g) * (ga * _sigmoid(ga))
    yc = _head_norm(cf_ref[0] + cb_ref[0], nc_ref[...], g) * _sigmoid(gc_ref[0])
    y = jnp.concatenate([ya.astype(BF16), yb_ref[0], yc.astype(BF16), yd_ref[0]], axis=-1)
    z = jnp.dot(y, w_ref[...], preferred_element_type=F32)

    def residual(row):
        gain = g_ref[1:2, :] * _mod_rows(mod_ref, 2, row, ctx_len)
        o_ref[0] = h_ref[0] + _unit_rms(z) * gain

    _per_tile_kind(tm, ctx_len, residual)


def _outproj(oa, ob, oc, od, u, h, mod, g4, na_row, nc_row, w, g, ctx_len, off):
    b, t, d = h.shape
    tm = _row_tile(t)
    gw = g * HEAD_DIM

    def rows(width, blk=0):
        return pl.BlockSpec((1, tm, width), lambda bi, i: (bi, i, blk))

    return pl.pallas_call(
        functools.partial(_outproj_kernel, g=g, ctx_len=ctx_len, tm=tm),
        name="outproj",
        grid=(b, t // tm),
        in_specs=[rows(gw), rows(gw), rows(gw, off["gdn_gate"] // gw), rows(gw),
                  rows(gw), rows(gw), rows(gw, off["ml_og"] // gw), rows(gw),
                  rows(d),
                  pl.BlockSpec((1, 2, 6, d), lambda bi, i: (bi, 0, 0, 0)),
                  pl.BlockSpec((4, d), lambda bi, i: (0, 0)),
                  pl.BlockSpec((1, gw), lambda bi, i: (0, 0)),
                  pl.BlockSpec((1, gw), lambda bi, i: (0, 0)),
                  pl.BlockSpec(w.shape, lambda bi, i: (0, 0))],
        out_specs=rows(d),
        out_shape=jax.ShapeDtypeStruct((b, t, d), F32),
        compiler_params=_params(("parallel", "parallel"), VMEM_LIMIT),
    )(oa[0], oa[1], u, ob, oc[0], oc[1], u, od, h, mod, g4, na_row, nc_row, w)


def _ffn_kernel(h_ref, mod_ref, g_ref, w1_ref, w2_ref, o_ref, xn_ref, acc_ref, *, ctx_len, tm):
    k = pl.program_id(2)

    def prenorm(row):
        gain = g_ref[2:3, :] * (1.0 + _mod_rows(mod_ref, 4, row, ctx_len))
        xn_ref[...] = (_unit_rms(h_ref[0]) * gain + _mod_rows(mod_ref, 3, row, ctx_len)).astype(BF16)
        acc_ref[...] = jnp.zeros_like(acc_ref)

    def residual(row):
        gain = g_ref[3:4, :] * _mod_rows(mod_ref, 5, row, ctx_len)
        o_ref[0] = h_ref[0] + _unit_rms(acc_ref[...]) * gain

    _per_tile_kind(tm, ctx_len, prenorm, also=k == 0)
    hid = jnp.maximum(jnp.dot(xn_ref[...], w1_ref[...], preferred_element_type=F32), 0.0)
    acc_ref[...] += jnp.dot((hid * hid).astype(BF16), w2_ref[...], preferred_element_type=F32)
    _per_tile_kind(tm, ctx_len, residual, also=k == pl.num_programs(2) - 1)


def _ffn(h, mod, g4, w1, w2, ctx_len):
    b, t, d = h.shape
    ff = w1.shape[1]
    tm = _row_tile(t)
    tf = _pick_tile(ff, (1024, 512, 256, 128))
    return pl.pallas_call(
        functools.partial(_ffn_kernel, ctx_len=ctx_len, tm=tm),
        name="ffn",
        grid=(b, t // tm, ff // tf),
        in_specs=[pl.BlockSpec((1, tm, d), lambda bi, i, k: (bi, i, 0)),
                  pl.BlockSpec((1, 2, 6, d), lambda bi, i, k: (bi, 0, 0, 0)),
                  pl.BlockSpec((4, d), lambda bi, i, k: (0, 0)),
                  pl.BlockSpec((d, tf), lambda bi, i, k: (0, k)),
                  pl.BlockSpec((tf, d), lambda bi, i, k: (k, 0))],
        out_specs=pl.BlockSpec((1, tm, d), lambda bi, i, k: (bi, i, 0)),
        out_shape=jax.ShapeDtypeStruct((b, t, d), F32),
        scratch_shapes=[pltpu.VMEM((tm, d), BF16), pltpu.VMEM((tm, d), F32)],
        compiler_params=_params(("parallel", "parallel", "arbitrary"), VMEM_LIMIT),
    )(h, mod, g4, w1, w2)


def _rope_tables(seq, ctx_len, dim):
    quarter = dim // 4
    t = jnp.arange(seq, dtype=jnp.int32)
    inv_freq = ROPE_BASE ** (-jnp.arange(quarter, dtype=F32) / quarter)
    ang_r = (t // GRID_W).astype(F32)[:, None] * inv_freq
    ang_c = (t % GRID_W).astype(F32)[:, None] * inv_freq
    zero = jnp.zeros_like(ang_r)
    cos = jnp.concatenate([jnp.cos(ang_r)] * 2 + [jnp.cos(ang_c)] * 2, axis=-1)
    s_up = jnp.concatenate([-jnp.sin(ang_r), zero, -jnp.sin(ang_c), zero], axis=-1)
    s_dn = jnp.concatenate([zero, jnp.sin(ang_r), zero, jnp.sin(ang_c)], axis=-1)

    def finish(tab, fill):
        tab = jnp.pad(tab, ((0, 0), (0, LANE - dim)), constant_values=fill)
        return jnp.pad(tab, ((ctx_len, 0), (0, 0)), constant_values=fill)

    return finish(cos, 1.0), finish(s_up, 0.0), finish(s_dn, 0.0)


def _lane_row(v):
    v = v.reshape(1, -1)
    return jnp.pad(v, ((0, 0), (0, LANE - v.shape[1])))


def kernel(x, c, ctx, c_ctx, w_mod, b_mod, g_norm, w_in, gdn_conv, gdn_a_log, gdn_dt_bias, gdn_norm,
           gqa_qk_norm, mlstm_gate_bias, mlstm_norm, mla_q_norm, mla_kv_norm, mla_w_qb, mla_w_kvb,
           w_out, w_ff1, w_ff2):
    bsz, seq, d = x.shape
    ctx_len = ctx.shape[1]
    depth = w_in.shape[0]
    g = w_out.shape[1] // (4 * HEAD_DIM)
    assert ctx_len % CHUNK == 0 and seq % CHUNK == 0 and seq % GRID_W == 0
    off, _ = _layout(g)
    cols, n_pad, n_src = _packed_columns(g)
    assert n_src == w_in.shape[2]

    cols = jnp.asarray(cols, jnp.int32)
    w_in_p = jnp.where(cols >= 0, jnp.take(w_in, jnp.maximum(cols, 0), axis=2), 0.0).astype(BF16)
    wq = mla_w_qb.reshape(depth, MLA_Q_RANK, g, MLA_NOPE + MLA_ROPE)
    wq = jnp.pad(wq, ((0, 0), (0, 0), (0, 0), (0, MLA_QK_PAD - MLA_NOPE - MLA_ROPE)))
    wq = wq.reshape(depth, MLA_Q_RANK, g * MLA_QK_PAD).astype(BF16)
    wkv = mla_w_kvb.reshape(depth, MLA_KV_RANK, g, MLA_NOPE + MLA_DV)
    wk = wkv[..., :MLA_NOPE].reshape(depth, MLA_KV_RANK, g * MLA_NOPE).astype(BF16)
    wv = wkv[..., MLA_NOPE:].reshape(depth, MLA_KV_RANK, g * MLA_DV).astype(BF16)
    w_out_b, w_ff1_b, w_ff2_b = w_out.astype(BF16), w_ff1.astype(BF16), w_ff2.astype(BF16)
    gqa_tab = _rope_tables(seq, ctx_len, HEAD_DIM)
    mla_tab = _rope_tables(seq, ctx_len, MLA_ROPE)

    n_rows = -(-(bsz + 1) // 8) * 8
    c_rows = jnp.concatenate([c, c_ctx[None, :], jnp.zeros((n_rows - bsz - 1, d), F32)], axis=0)
    mod_all = _mod_all(c_rows, w_mod, b_mod).reshape(depth, n_rows, 6, d)
    mod_all = jnp.stack([jnp.broadcast_to(mod_all[:, bsz:bsz + 1], (depth, bsz, 6, d)), mod_all[:, :bsz]], axis=2)

    h = jnp.concatenate([ctx, x], axis=1)
    for l in range(depth):
        mod, g4 = mod_all[l], g_norm[l]
        u = _inproj(h, mod, g4, w_in_p[l], ctx_len)
        qkv = _gdn_prep(u, gdn_conv[l], g, ctx_len, off)
        oa = _gdn_scan(qkv, u, _lane_row(gdn_a_log[l]), _lane_row(gdn_dt_bias[l]), g, ctx_len, off)
        qn, kn, vn = _gqa_prep(u, gqa_tab, gqa_qk_norm[l], g, off)
        ob = _attention(qn, kn, vn, g, g // 2, ctx_len)
        oc = _mlstm_scan(u, _lane_row(mlstm_gate_bias[l]), g, ctx_len, off)
        qm, km, vm = _mla_prep(u, mla_tab, mla_q_norm[l].reshape(1, -1), mla_kv_norm[l].reshape(1, -1),
                               wq[l], wk[l], wv[l], g, off)
        od = _attention(qm, km, vm, g, g, ctx_len)
        h = _outproj(oa, ob, oc, od, u, h, mod, g4, jnp.tile(gdn_norm[l], g).reshape(1, -1),
                     mlstm_norm[l].reshape(1, -1), w_out_b[l], g, ctx_len, off)
        h = _ffn(h, mod, g4, w_ff1_b[l], w_ff2_b[l], ctx_len)
    return h[:, ctx_len:, :]
```

```python
import functools
import math

import jax
import jax.numpy as jnp
from jax import lax
from jax.experimental import pallas as pl
from jax.experimental.pallas import tpu as pltpu

F32 = jnp.float32
BF16 = jnp.bfloat16

LANE = 128
HEAD_DIM = 128
CHUNK = 64
GRID_W = 64
ROPE_BASE = 10000.0
NORM_EPS = 1e-6
ML_DQK = 64
MLA_Q_RANK = 384
MLA_KV_RANK = 128
MLA_NOPE = 128
MLA_ROPE = 64
MLA_DV = 128
MLA_QK_PAD = 256
VMEM_LIMIT = 56 * 1024 * 1024
LOG2_E = math.log2(math.e)


def _dot(a, b):
    return jnp.dot(a.astype(BF16), b.astype(BF16), preferred_element_type=F32)


def _dot_nt(a, b):
    return lax.dot_general(a.astype(BF16), b.astype(BF16), (((1,), (1,)), ((), ())),
                           preferred_element_type=F32)


def _dot_tn(a, b):
    return lax.dot_general(a.astype(BF16), b.astype(BF16), (((0,), (0,)), ((), ())),
                           preferred_element_type=F32)


def _sigmoid(x):
    return 1.0 / (1.0 + jnp.exp(-x))


def _softplus(x):
    return jnp.maximum(x, 0.0) + jnp.log(1.0 + jnp.exp(-jnp.abs(x)))


def _rms(x, w):
    return x * lax.rsqrt(jnp.mean(x * x, axis=-1, keepdims=True) + NORM_EPS) * w


def _params(sem, vmem=None):
    return pltpu.CompilerParams(dimension_semantics=sem, vmem_limit_bytes=vmem)


def _layout(g):
    gw = g * HEAD_DIM
    order = [("mla_cqkv", MLA_Q_RANK + MLA_KV_RANK),
             ("gdn_q", gw), ("gdn_k", gw), ("gdn_v", gw), ("gdn_gate", gw),
             ("ml_q", gw), ("ml_k", gw), ("ml_v", gw), ("ml_og", gw), ("gqa_q", gw),
             ("gqa_k", gw // 2), ("gqa_v", gw // 2),
             ("gdn_ab", LANE), ("ml_if", LANE), ("mla_kpe", LANE)]
    off, pos = {}, 0
    for name, width in order:
        assert pos % width == 0, (name, pos, width)
        off[name] = pos
        pos += width
    return off, pos


def _packed_columns(g):
    gw = g * HEAD_DIM
    off, total = _layout(g)
    n_pad = -(-total // 512) * 512
    idx = [-1] * n_pad

    def put(name, src0, n, dst0=0):
        for t in range(n):
            idx[off[name] + dst0 + t] = src0 + t

    p = 0
    put("gdn_q", p, gw); p += gw
    put("gdn_k", p, gw); p += gw
    put("gdn_v", p, gw); p += gw
    put("gdn_gate", p, gw); p += gw
    put("gdn_ab", p, 4 * g); p += 4 * g
    put("gqa_q", p, gw); p += gw
    put("gqa_k", p, gw // 2); p += gw // 2
    put("gqa_v", p, gw // 2); p += gw // 2
    for name in ("ml_q", "ml_k"):
        for h in range(g):
            put(name, p + h * ML_DQK, ML_DQK, h * HEAD_DIM)
        p += g * ML_DQK
    put("ml_v", p, gw); p += gw
    put("ml_og", p, gw); p += gw
    put("ml_if", p, 4 * g); p += 4 * g
    put("mla_cqkv", p, MLA_Q_RANK + MLA_KV_RANK); p += MLA_Q_RANK + MLA_KV_RANK
    put("mla_kpe", p, MLA_ROPE); p += MLA_ROPE
    return idx, n_pad, p


def _pack_columns(w, cols):
    parts, start = [], 0
    for i in range(1, len(cols) + 1):
        run_ends = i == len(cols) or (cols[i] != cols[i - 1] + 1 if cols[i - 1] >= 0 else cols[i] >= 0)
        if run_ends:
            if cols[start] >= 0:
                parts.append(w[..., cols[start]:cols[start] + i - start])
            else:
                parts.append(jnp.zeros(w.shape[:-1] + (i - start,), w.dtype))
            start = i
    return jnp.concatenate(parts, axis=-1)


def _pick_tile(n, candidates):
    for c in candidates:
        if n % c == 0:
            return c
    raise ValueError(f"no tile for {n} in {candidates}")


def _row_tile(t, limit=640):
    for c in range(limit - limit % 16, 15, -16):
        if t % c == 0:
            return c
    raise ValueError(f"no row tile for {t}")


def _mod_kernel(c_ref, w_ref, b_ref, o_ref):
    c = c_ref[...]
    o_ref[0] = _dot(c * _sigmoid(c), w_ref[0]) + b_ref[0]


def _mod_all(c_rows, w_mod, b_mod):
    depth, d, n = w_mod.shape
    r = c_rows.shape[0]
    tn = _pick_tile(n, (1024, 512, 256, 128))
    return pl.pallas_call(
        _mod_kernel,
        name="mod",
        grid=(depth, n // tn),
        in_specs=[pl.BlockSpec((r, d), lambda l, j: (0, 0)),
                  pl.BlockSpec((1, d, tn), lambda l, j: (l, 0, j)),
                  pl.BlockSpec((1, 1, tn), lambda l, j: (l, 0, j))],
        out_specs=pl.BlockSpec((1, r, tn), lambda l, j: (l, 0, j)),
        out_shape=jax.ShapeDtypeStruct((depth, r, n), F32),
        compiler_params=_params(("parallel", "parallel"), VMEM_LIMIT),
    )(c_rows, w_mod, b_mod.reshape(depth, 1, n))


def _mod_rows(mod_ref, idx, row, ctx_len):
    if row is None:
        return mod_ref[0, 1, idx:idx + 1, :]
    return jnp.where(row < ctx_len, mod_ref[0, 0, idx:idx + 1, :], mod_ref[0, 1, idx:idx + 1, :])


def _unit_rms(x):
    return x * lax.rsqrt(jnp.mean(x * x, axis=-1, keepdims=True) + NORM_EPS)


def _per_tile_kind(tm, ctx_len, body, also=True):
    first = pl.program_id(1) * tm

    @pl.when(jnp.logical_and(also, first >= ctx_len))
    def _():
        body(None)

    @pl.when(jnp.logical_and(also, first < ctx_len))
    def _():
        body(first + lax.broadcasted_iota(jnp.int32, (tm, 1), 0))


def _prenorm_kernel(h_ref, mod_ref, g_ref, o_ref, *, ctx_len, tm):
    def body(row):
        gain = g_ref[0:1, :] * (1.0 + _mod_rows(mod_ref, 1, row, ctx_len))
        o_ref[0] = (_unit_rms(h_ref[0]) * gain + _mod_rows(mod_ref, 0, row, ctx_len)).astype(BF16)

    _per_tile_kind(tm, ctx_len, body)


def _inproj_kernel(x_ref, w_ref, o_ref):
    o_ref[0] = jnp.dot(x_ref[0], w_ref[...], preferred_element_type=F32)


def _inproj(h, mod, g4, w, ctx_len):
    b, t, d = h.shape
    n = w.shape[1]
    tm = _row_tile(t)
    xn = pl.pallas_call(
        functools.partial(_prenorm_kernel, ctx_len=ctx_len, tm=tm),
        name="prenorm",
        grid=(b, t // tm),
        in_specs=[pl.BlockSpec((1, tm, d), lambda bi, i: (bi, i, 0)),
                  pl.BlockSpec((1, 2, 6, d), lambda bi, i: (bi, 0, 0, 0)),
                  pl.BlockSpec((4, d), lambda bi, i: (0, 0))],
        out_specs=pl.BlockSpec((1, tm, d), lambda bi, i: (bi, i, 0)),
        out_shape=jax.ShapeDtypeStruct((b, t, d), BF16),
        compiler_params=_params(("parallel", "parallel"), VMEM_LIMIT),
    )(h, mod, g4)
    tm = _row_tile(t, 1280)
    tn = _pick_tile(n, (1536, 1024, 512))
    return pl.pallas_call(
        _inproj_kernel,
        name="inproj",
        grid=(n // tn, b, t // tm),
        in_specs=[pl.BlockSpec((1, tm, d), lambda j, bi, i: (bi, i, 0)),
                  pl.BlockSpec((d, tn), lambda j, bi, i: (0, j))],
        out_specs=pl.BlockSpec((1, tm, tn), lambda j, bi, i: (bi, i, j)),
        out_shape=jax.ShapeDtypeStruct((b, t, n), F32),
        compiler_params=_params(("parallel", "parallel", "parallel"), VMEM_LIMIT),
    )(xn, w)


def _gdn_prep_kernel(u_ref, w_ref, o_ref, *, g, ctx_len):
    j = pl.program_id(1)
    x = u_ref[0]
    t = x.shape[0]
    row = lax.broadcasted_iota(jnp.int32, (t, 1), 0)
    first = (row == 0) | (row == ctx_len)
    last = (row == ctx_len - 1) | (row == t - 1)
    x_prev = jnp.where(first, 0.0, pltpu.roll(x, 1, 0))
    x_next = jnp.where(last, 0.0, pltpu.roll(x, t - 1, 0))
    y = x_prev * w_ref[0:1, :] + x * w_ref[1:2, :] + x_next * w_ref[2:3, :]
    y = y * _sigmoid(y)
    yn = y * lax.rsqrt(jnp.sum(y * y, axis=-1, keepdims=True) + NORM_EPS)
    scale = jnp.where(j < g, HEAD_DIM ** -0.5, 1.0)
    o_ref[0] = jnp.where(j < 2 * g, yn * scale, y)


def _gdn_prep(u, conv_w, g, ctx_len, off):
    b, t, _ = u.shape
    blk0 = off["gdn_q"] // LANE
    return pl.pallas_call(
        functools.partial(_gdn_prep_kernel, g=g, ctx_len=ctx_len),
        name="gdn_prep",
        grid=(b, 3 * g),
        in_specs=[pl.BlockSpec((1, t, LANE), lambda bi, j: (bi, 0, blk0 + j)),
                  pl.BlockSpec((3, LANE), lambda bi, j: (0, j))],
        out_specs=pl.BlockSpec((1, t, LANE), lambda bi, j: (bi, 0, j)),
        out_shape=jax.ShapeDtypeStruct((b, t, 3 * g * LANE), F32),
        compiler_params=_params(("parallel", "parallel"), VMEM_LIMIT),
    )(u, conv_w)


def _chunk_masks(fwd):
    ri = lax.broadcasted_iota(jnp.int32, (CHUNK, CHUNK), 0)
    ci = lax.broadcasted_iota(jnp.int32, (CHUNK, CHUNK), 1)
    if fwd:
        return ri >= ci, ri > ci, ri == ci
    return ri <= ci, ri < ci, ri == ci


def _bcast(col):
    return jnp.broadcast_to(col, (CHUNK, LANE))


def _split3(x):
    x1 = x.astype(BF16)
    r = x - x1.astype(F32)
    x2 = r.astype(BF16)
    return x1, x2, (r - x2.astype(F32)).astype(BF16)


def _chunk_cumsum(x, incl):
    lmat = jnp.where(incl, 1.0, 0.0).astype(BF16)
    p1, p2, p3 = _split3(x)
    return (jnp.dot(lmat, p1, preferred_element_type=F32) + jnp.dot(lmat, p2, preferred_element_type=F32)
            + jnp.dot(lmat, p3, preferred_element_type=F32))


def _pair_diff(g_b):
    return g_b[:, :CHUNK] - g_b.T[:CHUNK, :]


def _tri_inverse_all(a_list, diag):
    ri = lax.broadcasted_iota(jnp.int32, (CHUNK, CHUNK), 0)
    ci = lax.broadcasted_iota(jnp.int32, (CHUNK, CHUNK), 1)
    eye = diag.astype(F32)
    ts = [jnp.where((ri >> 1) == (ci >> 1), eye - a, 0.0) for a in a_list]
    for lb in range(1, int(math.log2(CHUNK))):
        join = ((ri >> (lb + 1)) == (ci >> (lb + 1))) & ((ri >> lb) != (ci >> lb))
        us = [_dot(jnp.where(join, a, 0.0), t) for a, t in zip(a_list, ts)]
        ts = [t - _dot(t, u) for t, u in zip(ts, us)]
    return ts


def _gdn_scan_kernel(qf_ref, gf_ref, qb_ref, gb_ref, al_ref, dt_ref, of_ref, ob_ref, s_ref, *, g, nb):
    @pl.when(pl.program_id(1) == 0)
    def _():
        s_ref[...] = jnp.zeros_like(s_ref)

    gw = g * HEAD_DIM
    streams = range(nb * 2 * g)
    fwd, incl, strict, q, k, v, g_b, beta_b, dest = [], [], [], [], [], [], [], [], []
    diag = _chunk_masks(True)[2]
    for bb in range(nb):
        for d, (q_ref, gt_ref) in enumerate(((qf_ref, gf_ref), (qb_ref, gb_ref))):
            m_incl, m_strict, _ = _chunk_masks(d == 0)
            gates = gt_ref[bb]
            la = -jnp.exp(al_ref[...]) * _softplus(gates + dt_ref[...])
            g_all = _chunk_cumsum(la, m_incl)
            be = _sigmoid(gates)
            for h in range(g):
                c = d * g + h
                fwd.append(d == 0)
                incl.append(m_incl)
                strict.append(m_strict)
                dest.append((of_ref if d == 0 else ob_ref, bb, h))
                q.append(q_ref[bb, :, h * HEAD_DIM:(h + 1) * HEAD_DIM])
                k.append(q_ref[bb, :, gw + h * HEAD_DIM:gw + (h + 1) * HEAD_DIM])
                v.append(q_ref[bb, :, 2 * gw + h * HEAD_DIM:2 * gw + (h + 1) * HEAD_DIM])
                g_b.append(_bcast(g_all[:, c:c + 1]))
                beta_b.append(_bcast(be[:, 2 * g + c:2 * g + c + 1]))

    decay = [jnp.exp(jnp.where(incl[i], _pair_diff(g_b[i]), -jnp.inf)) for i in streams]
    kb = [k[i] * beta_b[i] for i in streams]
    kq = [_dot_nt(jnp.concatenate([kb[i], q[i]], axis=0), k[i]) for i in streams]
    a = [jnp.where(strict[i], kq[i][:CHUNK] * decay[i], 0.0) for i in streams]
    attn = [kq[i][CHUNK:] * decay[i] for i in streams]
    t = _tri_inverse_all(a, diag)
    eg = [jnp.exp(g_b[i]) for i in streams]
    vw = [_dot(t[i], jnp.concatenate([v[i] * beta_b[i], kb[i] * eg[i]], axis=1)) for i in streams]
    s = [s_ref[i] for i in streams]
    ws = [_dot(jnp.concatenate([vw[i][:, HEAD_DIM:], q[i] * eg[i]], axis=0), s[i]) for i in streams]
    v_new = [vw[i][:, :HEAD_DIM] - ws[i][:CHUNK] for i in streams]
    o = [ws[i][CHUNK:] + _dot(attn[i], v_new[i]) for i in streams]
    g_end = [g_b[i][CHUNK - 1:CHUNK, :] if fwd[i] else g_b[i][0:1, :] for i in streams]
    s_new = [s[i] * jnp.exp(g_end[i]) + _dot_tn(k[i] * jnp.exp(g_end[i] - g_b[i]), v_new[i]) for i in streams]
    for i in streams:
        o_ref, bb, h = dest[i]
        o_ref[bb, :, h * HEAD_DIM:(h + 1) * HEAD_DIM] = o[i]
        s_ref[i] = s_new[i]


SCAN_BATCH = 4


def _scan_batch(b):
    return math.gcd(b, SCAN_BATCH)


def _scan_chunk_maps(n_ctx, n_all):
    def fwd(bi, s):
        return (bi, s, 0)

    def bwd(bi, s):
        return (bi, jnp.where(s < n_ctx, n_ctx - 1 - s, n_all + n_ctx - 1 - s), 0)

    return fwd, bwd


def _gdn_scan(qkv, u, a_log_row, dt_row, g, ctx_len, off):
    b, t, _ = qkv.shape
    n_all, n_ctx = t // CHUNK, ctx_len // CHUNK
    fwd, bwd = _scan_chunk_maps(n_ctx, n_all)
    gblk = off["gdn_ab"] // LANE

    def gate_map(m):
        return lambda bi, s: (m(bi, s)[0], m(bi, s)[1], gblk)

    gw = g * HEAD_DIM
    nb = _scan_batch(b)
    return pl.pallas_call(
        functools.partial(_gdn_scan_kernel, g=g, nb=nb),
        name="gdn_scan",
        grid=(b // nb, n_all),
        in_specs=[pl.BlockSpec((nb, CHUNK, 3 * gw), fwd),
                  pl.BlockSpec((nb, CHUNK, LANE), gate_map(fwd)),
                  pl.BlockSpec((nb, CHUNK, 3 * gw), bwd),
                  pl.BlockSpec((nb, CHUNK, LANE), gate_map(bwd)),
                  pl.BlockSpec((1, LANE), lambda bi, s: (0, 0)),
                  pl.BlockSpec((1, LANE), lambda bi, s: (0, 0))],
        out_specs=[pl.BlockSpec((nb, CHUNK, gw), fwd), pl.BlockSpec((nb, CHUNK, gw), bwd)],
        out_shape=[jax.ShapeDtypeStruct((b, t, gw), F32)] * 2,
        scratch_shapes=[pltpu.VMEM((nb * 2 * g, HEAD_DIM, HEAD_DIM), F32)],
        compiler_params=_params(("parallel", "arbitrary")),
    )(qkv, u, qkv, u, a_log_row, dt_row)


def _mlstm_scan_kernel(qf_ref, kf_ref, vf_ref, gf_ref, qb_ref, kb_ref, vb_ref, gb_ref, bias_ref,
                       of_ref, ob_ref, c_ref, n_ref, m_ref, *, g, nb):
    @pl.when(pl.program_id(1) == 0)
    def _():
        c_ref[...] = jnp.zeros_like(c_ref)
        n_ref[...] = jnp.zeros_like(n_ref)
        m_ref[...] = jnp.zeros_like(m_ref)

    streams = range(nb * 2 * g)
    fwd, incl, q, k, v, b_b, ip_b, dest = [], [], [], [], [], [], [], []
    dirs = ((qf_ref, kf_ref, vf_ref, gf_ref), (qb_ref, kb_ref, vb_ref, gb_ref))
    for bb in range(nb):
        for d, (q_ref, k_ref, v_ref, gt_ref) in enumerate(dirs):
            m_incl = _chunk_masks(d == 0)[0]
            gates = gt_ref[bb] + bias_ref[...]
            logsig = jnp.minimum(gates, 0.0) - jnp.log(1.0 + jnp.exp(-jnp.abs(gates)))
            b_all = _chunk_cumsum(logsig, m_incl)
            for h in range(g):
                c = d * g + h
                sl = slice(h * HEAD_DIM, (h + 1) * HEAD_DIM)
                fwd.append(d == 0)
                incl.append(m_incl)
                dest.append((of_ref if d == 0 else ob_ref, bb, h))
                q.append(q_ref[bb, :, sl] * (ML_DQK ** -0.5))
                k.append(k_ref[bb, :, sl])
                v.append(v_ref[bb, :, sl])
                b_b.append(_bcast(b_all[:, 2 * g + c:2 * g + c + 1]))
                ip_b.append(_bcast(gates[:, c:c + 1]))

    dmat = [jnp.where(incl[i], _pair_diff(b_b[i]) + ip_b[i].T[:CHUNK, :], -jnp.inf) for i in streams]
    qk = [_dot_nt(q[i], k[i]) for i in streams]
    cm = [c_ref[i] for i in streams]
    nrow = [n_ref[i] for i in streams]
    mrow = [m_ref[i] for i in streams]
    b_end = [b_b[i][CHUNK - 1:CHUNK, :] if fwd[i] else b_b[i][0:1, :] for i in streams]
    w_b = [b_end[i] - b_b[i] + ip_b[i] for i in streams]
    inter = [b_b[i] + mrow[i] for i in streams]
    m_t = [jnp.maximum(inter[i], jnp.max(dmat[i], axis=1, keepdims=True)) for i in streams]
    s_inter = [jnp.exp(inter[i] - m_t[i]) for i in streams]
    p = [jnp.exp(dmat[i] - m_t[i][:, :CHUNK]) * qk[i] for i in streams]
    num = [s_inter[i] * _dot(q[i], cm[i]) + _dot(p[i], v[i]) for i in streams]
    den = [s_inter[i] * jnp.sum(q[i] * nrow[i], axis=1, keepdims=True) + jnp.sum(p[i], axis=1, keepdims=True)
           for i in streams]
    hh = [num[i] / jnp.maximum(jnp.abs(den[i]), jnp.exp(-m_t[i])) for i in streams]
    m_new = [jnp.maximum(b_end[i] + mrow[i], jnp.max(w_b[i], axis=0, keepdims=True)) for i in streams]
    s_old = [jnp.exp(b_end[i] + mrow[i] - m_new[i]) for i in streams]
    wk = [jnp.exp(w_b[i] - m_new[i]) * k[i] for i in streams]
    cm_new = [s_old[i] * cm[i] + _dot_tn(wk[i], v[i]) for i in streams]
    n_new = [s_old[i] * nrow[i] + jnp.sum(wk[i], axis=0, keepdims=True) for i in streams]
    for i in streams:
        o_ref, bb, h = dest[i]
        o_ref[bb, :, h * HEAD_DIM:(h + 1) * HEAD_DIM] = hh[i]
        c_ref[i] = cm_new[i]
        n_ref[i] = n_new[i]
        m_ref[i] = m_new[i]


def _mlstm_scan(u, bias_row, g, ctx_len, off):
    b, t, _ = u.shape
    n_all, n_ctx = t // CHUNK, ctx_len // CHUNK
    fwd, bwd = _scan_chunk_maps(n_ctx, n_all)
    gw = g * HEAD_DIM

    def col_map(m, blk):
        return lambda bi, s: (m(bi, s)[0], m(bi, s)[1], blk)

    nb = _scan_batch(b)

    def specs(m):
        return [pl.BlockSpec((nb, CHUNK, gw), col_map(m, off["ml_q"] // gw)),
                pl.BlockSpec((nb, CHUNK, gw), col_map(m, off["ml_k"] // gw)),
                pl.BlockSpec((nb, CHUNK, gw), col_map(m, off["ml_v"] // gw)),
                pl.BlockSpec((nb, CHUNK, LANE), col_map(m, off["ml_if"] // LANE))]

    return pl.pallas_call(
        functools.partial(_mlstm_scan_kernel, g=g, nb=nb),
        name="mlstm_scan",
        grid=(b // nb, n_all),
        in_specs=specs(fwd) + specs(bwd) + [pl.BlockSpec((1, LANE), lambda bi, s: (0, 0))],
        out_specs=[pl.BlockSpec((nb, CHUNK, gw), fwd), pl.BlockSpec((nb, CHUNK, gw), bwd)],
        out_shape=[jax.ShapeDtypeStruct((b, t, gw), F32)] * 2,
        scratch_shapes=[pltpu.VMEM((nb * 2 * g, HEAD_DIM, HEAD_DIM), F32),
                        pltpu.VMEM((nb * 2 * g, 1, HEAD_DIM), F32),
                        pltpu.VMEM((nb * 2 * g, 1, HEAD_DIM), F32)],
        compiler_params=_params(("parallel", "arbitrary")),
    )(u, u, u, u, u, u, u, u, bias_row)


def _rope(x, c, s_up, s_dn, shift):
    return x * c + pltpu.roll(x, LANE - shift, 1) * s_up + pltpu.roll(x, shift, 1) * s_dn


def _gqa_prep_kernel(q_ref, k_ref, v_ref, c_ref, su_ref, sd_ref, n_ref, qo_ref, ko_ref, vo_ref, *, g):
    c, su, sd = c_ref[...], su_ref[...], sd_ref[...]
    scale = HEAD_DIM ** -0.5 * LOG2_E
    for h in range(g):
        sl = slice(h * HEAD_DIM, (h + 1) * HEAD_DIM)
        qo_ref[0, :, sl] = (_rope(_rms(q_ref[0, :, sl], n_ref[0:1, :]), c, su, sd, 32) * scale).astype(BF16)
    for h in range(g // 2):
        sl = slice(h * HEAD_DIM, (h + 1) * HEAD_DIM)
        ko_ref[0, :, sl] = _rope(_rms(k_ref[0, :, sl], n_ref[1:2, :]), c, su, sd, 32).astype(BF16)
    vo_ref[0] = v_ref[0].astype(BF16)


def _gqa_prep(u, tables, qk_norm, g, off):
    b, t, _ = u.shape
    tm = _row_tile(t)
    gw, kw = g * HEAD_DIM, g * HEAD_DIM // 2
    tab = pl.BlockSpec((tm, LANE), lambda bi, i: (i, 0))
    return pl.pallas_call(
        functools.partial(_gqa_prep_kernel, g=g),
        name="gqa_prep",
        grid=(b, t // tm),
        in_specs=[pl.BlockSpec((1, tm, gw), lambda bi, i: (bi, i, off["gqa_q"] // gw)),
                  pl.BlockSpec((1, tm, kw), lambda bi, i: (bi, i, off["gqa_k"] // kw)),
                  pl.BlockSpec((1, tm, kw), lambda bi, i: (bi, i, off["gqa_v"] // kw)),
                  tab, tab, tab,
                  pl.BlockSpec((2, HEAD_DIM), lambda bi, i: (0, 0))],
        out_specs=[pl.BlockSpec((1, tm, gw), lambda bi, i: (bi, i, 0)),
                   pl.BlockSpec((1, tm, kw), lambda bi, i: (bi, i, 0)),
                   pl.BlockSpec((1, tm, kw), lambda bi, i: (bi, i, 0))],
        out_shape=[jax.ShapeDtypeStruct((b, t, gw), BF16),
                   jax.ShapeDtypeStruct((b, t, kw), BF16),
                   jax.ShapeDtypeStruct((b, t, kw), BF16)],
        compiler_params=_params(("parallel", "parallel")),
    )(u, u, u, *tables, qk_norm)


def _mla_prep_kernel(x_ref, kpe_ref, c_ref, su_ref, sd_ref, qn_ref, kvn_ref, wq_ref, wk_ref, wv_ref,
                     qo_ref, ko_ref, vo_ref, *, g):
    c, su, sd = c_ref[...], su_ref[...], sd_ref[...]
    scale = (MLA_NOPE + MLA_ROPE) ** -0.5 * LOG2_E
    x = x_ref[0]
    cq = _rms(x[:, :MLA_Q_RANK], qn_ref[...])
    ckv = _rms(x[:, MLA_Q_RANK:], kvn_ref[...])
    q = _dot(cq, wq_ref[...])
    kn = _dot(ckv, wk_ref[...])
    vo_ref[0] = _dot(ckv, wv_ref[...]).astype(BF16)
    kpe = _rope(kpe_ref[0], c, su, sd, 16).astype(BF16)
    for h in range(g):
        lo = h * MLA_QK_PAD
        qo_ref[0, :, lo:lo + LANE] = (q[:, lo:lo + LANE] * scale).astype(BF16)
        qo_ref[0, :, lo + LANE:lo + 2 * LANE] = (
            _rope(q[:, lo + LANE:lo + 2 * LANE], c, su, sd, 16) * scale).astype(BF16)
        ko_ref[0, :, lo:lo + LANE] = kn[:, h * LANE:(h + 1) * LANE].astype(BF16)
        ko_ref[0, :, lo + LANE:lo + 2 * LANE] = kpe


def _mla_prep(u, tables, q_norm, kv_norm, wq, wk, wv, g, off):
    b, t, _ = u.shape
    tm = _row_tile(t)
    cw = MLA_Q_RANK + MLA_KV_RANK
    tab = pl.BlockSpec((tm, LANE), lambda bi, i: (i, 0))

    def full(a):
        return pl.BlockSpec(a.shape, lambda bi, i: (0,) * a.ndim)

    return pl.pallas_call(
        functools.partial(_mla_prep_kernel, g=g),
        name="mla_prep",
        grid=(b, t // tm),
        in_specs=[pl.BlockSpec((1, tm, cw), lambda bi, i: (bi, i, off["mla_cqkv"] // cw)),
                  pl.BlockSpec((1, tm, LANE), lambda bi, i: (bi, i, off["mla_kpe"] // LANE)),
                  tab, tab, tab, full(q_norm), full(kv_norm), full(wq), full(wk), full(wv)],
        out_specs=[pl.BlockSpec((1, tm, g * MLA_QK_PAD), lambda bi, i: (bi, i, 0)),
                   pl.BlockSpec((1, tm, g * MLA_QK_PAD), lambda bi, i: (bi, i, 0)),
                   pl.BlockSpec((1, tm, g * MLA_DV), lambda bi, i: (bi, i, 0))],
        out_shape=[jax.ShapeDtypeStruct((b, t, g * MLA_QK_PAD), BF16),
                   jax.ShapeDtypeStruct((b, t, g * MLA_QK_PAD), BF16),
                   jax.ShapeDtypeStruct((b, t, g * MLA_DV), BF16)],
        compiler_params=_params(("parallel", "parallel")),
    )(u, u, *tables, q_norm, kv_norm, wq, wk, wv)


KV_HEADS_PER_STEP = 2


def _attn_kernel(q_ref, k_ref, v_ref, o_ref, *, ctx_len, tq, hp, rep, dq, dv):
    heads = range(hp)

    def attend(nk):
        s = [lax.dot_general(q_ref[0, :, j * dq:(j + 1) * dq], k_ref[0, :nk, (j // rep) * dq:(j // rep + 1) * dq],
                             (((1,), (1,)), ((), ())), preferred_element_type=F32) for j in heads]
        p = [jnp.exp2(s[j] - jnp.max(s[j], axis=-1, keepdims=True)) for j in heads]
        l = [jnp.sum(p[j], axis=-1, keepdims=True) for j in heads]
        o = [jnp.dot(p[j].astype(BF16), v_ref[0, :nk, (j // rep) * dv:(j // rep + 1) * dv],
                     preferred_element_type=F32) for j in heads]
        for j in heads:
            o_ref[0, :, j * dv:(j + 1) * dv] = (o[j] / l[j]).astype(o_ref.dtype)

    is_ctx = pl.program_id(2) * tq < ctx_len

    @pl.when(is_ctx)
    def _():
        attend(ctx_len)

    @pl.when(jnp.logical_not(is_ctx))
    def _():
        attend(k_ref.shape[1])


def _attention(q, k, v, heads, kv_heads, ctx_len):
    b, t, _ = q.shape
    dq, dv = q.shape[2] // heads, v.shape[2] // kv_heads
    rep = heads // kv_heads
    kvp = math.gcd(kv_heads, KV_HEADS_PER_STEP)
    hp = kvp * rep
    tq = math.gcd(256, ctx_len)
    return pl.pallas_call(
        functools.partial(_attn_kernel, ctx_len=ctx_len, tq=tq, hp=hp, rep=rep, dq=dq, dv=dv),
        name="attn",
        grid=(b, heads // hp, t // tq),
        in_specs=[pl.BlockSpec((1, tq, hp * dq), lambda bi, h, i: (bi, i, h)),
                  pl.BlockSpec((1, t, kvp * dq), lambda bi, h, i: (bi, 0, h * hp // (rep * kvp))),
                  pl.BlockSpec((1, t, kvp * dv), lambda bi, h, i: (bi, 0, h * hp // (rep * kvp)))],
        out_specs=pl.BlockSpec((1, tq, hp * dv), lambda bi, h, i: (bi, i, h)),
        out_shape=jax.ShapeDtypeStruct((b, t, heads * dv), BF16),
        compiler_params=_params(("parallel", "parallel", "parallel"), VMEM_LIMIT),
    )(q, k, v)


def _head_norm(x, w_row, g):
    parts = []
    for h in range(g):
        xs = x[:, h * HEAD_DIM:(h + 1) * HEAD_DIM]
        parts.append(xs * lax.rsqrt(jnp.mean(xs * xs, axis=-1, keepdims=True) + NORM_EPS))
    return jnp.concatenate(parts, axis=-1) * w_row


def _outproj_kernel(af_ref, ab_ref, ga_ref, yb_ref, cf_ref, cb_ref, gc_ref, yd_ref, h_ref, mod_ref, g_ref,
                    na_ref, nc_ref, w_ref, o_ref, *, g, ctx_len, tm):
    ga = ga_ref[0]
    ya = _head_norm(af_ref[0] + ab_ref[0], na_ref[...], g) * (ga * _sigmoid(ga))
    yc = _head_norm(cf_ref[0] + cb_ref[0], nc_ref[...], g) * _sigmoid(gc_ref[0])
    y = jnp.concatenate([ya.astype(BF16), yb_ref[0], yc.astype(BF16), yd_ref[0]], axis=-1)
    z = jnp.dot(y, w_ref[...], preferred_element_type=F32)

    def residual(row):
        gain = g_ref[1:2, :] * _mod_rows(mod_ref, 2, row, ctx_len)
        o_ref[0] = h_ref[0] + _unit_rms(z) * gain

    _per_tile_kind(tm, ctx_len, residual)


def _outproj(oa, ob, oc, od, u, h, mod, g4, na_row, nc_row, w, g, ctx_len, off):
    b, t, d = h.shape
    tm = _row_tile(t)
    gw = g * HEAD_DIM

    def rows(width, blk=0):
        return pl.BlockSpec((1, tm, width), lambda bi, i: (bi, i, blk))

    return pl.pallas_call(
        functools.partial(_outproj_kernel, g=g, ctx_len=ctx_len, tm=tm),
        name="outproj",
        grid=(b, t // tm),
        in_specs=[rows(gw), rows(gw), rows(gw, off["gdn_gate"] // gw), rows(gw),
                  rows(gw), rows(gw), rows(gw, off["ml_og"] // gw), rows(gw),
                  rows(d),
                  pl.BlockSpec((1, 2, 6, d), lambda bi, i: (bi, 0, 0, 0)),
                  pl.BlockSpec((4, d), lambda bi, i: (0, 0)),
                  pl.BlockSpec((1, gw), lambda bi, i: (0, 0)),
                  pl.BlockSpec((1, gw), lambda bi, i: (0, 0)),
                  pl.BlockSpec(w.shape, lambda bi, i: (0, 0))],
        out_specs=rows(d),
        out_shape=jax.ShapeDtypeStruct((b, t, d), F32),
        compiler_params=_params(("parallel", "parallel"), VMEM_LIMIT),
    )(oa[0], oa[1], u, ob, oc[0], oc[1], u, od, h, mod, g4, na_row, nc_row, w)


def _ffn_kernel(h_ref, mod_ref, g_ref, w1_ref, w2_ref, o_ref, xn_ref, acc_ref, *, ctx_len, tm):
    k = pl.program_id(2)

    def prenorm(row):
        gain = g_ref[2:3, :] * (1.0 + _mod_rows(mod_ref, 4, row, ctx_len))
        xn_ref[...] = (_unit_rms(h_ref[0]) * gain + _mod_rows(mod_ref, 3, row, ctx_len)).astype(BF16)
        acc_ref[...] = jnp.zeros_like(acc_ref)

    def residual(row):
        gain = g_ref[3:4, :] * _mod_rows(mod_ref, 5, row, ctx_len)
        o_ref[0] = h_ref[0] + _unit_rms(acc_ref[...]) * gain

    _per_tile_kind(tm, ctx_len, prenorm, also=k == 0)
    hid = jnp.maximum(jnp.dot(xn_ref[...], w1_ref[...], preferred_element_type=F32), 0.0)
    acc_ref[...] += jnp.dot((hid * hid).astype(BF16), w2_ref[...], preferred_element_type=F32)
    _per_tile_kind(tm, ctx_len, residual, also=k == pl.num_programs(2) - 1)


def _ffn(h, mod, g4, w1, w2, ctx_len):
    b, t, d = h.shape
    ff = w1.shape[1]
    tm = _row_tile(t)
    tf = _pick_tile(ff, (1024, 512, 256, 128))
    return pl.pallas_call(
        functools.partial(_ffn_kernel, ctx_len=ctx_len, tm=tm),
        name="ffn",
        grid=(b, t // tm, ff // tf),
        in_specs=[pl.BlockSpec((1, tm, d), lambda bi, i, k: (bi, i, 0)),
                  pl.BlockSpec((1, 2, 6, d), lambda bi, i, k: (bi, 0, 0, 0)),
                  pl.BlockSpec((4, d), lambda bi, i, k: (0, 0)),
                  pl.BlockSpec((d, tf), lambda bi, i, k: (0, k)),
                  pl.BlockSpec((tf, d), lambda bi, i, k: (k, 0))],
        out_specs=pl.BlockSpec((1, tm, d), lambda bi, i, k: (bi, i, 0)),
        out_shape=jax.ShapeDtypeStruct((b, t, d), F32),
        scratch_shapes=[pltpu.VMEM((tm, d), BF16), pltpu.VMEM((tm, d), F32)],
        compiler_params=_params(("parallel", "parallel", "arbitrary"), VMEM_LIMIT),
    )(h, mod, g4, w1, w2)


def _rope_tables(seq, ctx_len, dim):
    quarter = dim // 4
    t = jnp.arange(seq, dtype=jnp.int32)
    inv_freq = ROPE_BASE ** (-jnp.arange(quarter, dtype=F32) / quarter)
    ang_r = (t // GRID_W).astype(F32)[:, None] * inv_freq
    ang_c = (t % GRID_W).astype(F32)[:, None] * inv_freq
    zero = jnp.zeros_like(ang_r)
    cos = jnp.concatenate([jnp.cos(ang_r)] * 2 + [jnp.cos(ang_c)] * 2, axis=-1)
    s_up = jnp.concatenate([-jnp.sin(ang_r), zero, -jnp.sin(ang_c), zero], axis=-1)
    s_dn = jnp.concatenate([zero, jnp.sin(ang_r), zero, jnp.sin(ang_c)], axis=-1)

    def finish(tab, fill):
        tab = jnp.pad(tab, ((0, 0), (0, LANE - dim)), constant_values=fill)
        return jnp.pad(tab, ((ctx_len, 0), (0, 0)), constant_values=fill)

    return finish(cos, 1.0), finish(s_up, 0.0), finish(s_dn, 0.0)


def _lane_row(v):
    v = v.reshape(1, -1)
    return jnp.pad(v, ((0, 0), (0, LANE - v.shape[1])))


def kernel(x, c, ctx, c_ctx, w_mod, b_mod, g_norm, w_in, gdn_conv, gdn_a_log, gdn_dt_bias, gdn_norm,
           gqa_qk_norm, mlstm_gate_bias, mlstm_norm, mla_q_norm, mla_kv_norm, mla_w_qb, mla_w_kvb,
           w_out, w_ff1, w_ff2):
    bsz, seq, d = x.shape
    ctx_len = ctx.shape[1]
    depth = w_in.shape[0]
    g = w_out.shape[1] // (4 * HEAD_DIM)
    assert ctx_len % CHUNK == 0 and seq % CHUNK == 0 and seq % GRID_W == 0
    off, _ = _layout(g)
    cols, n_pad, n_src = _packed_columns(g)
    assert n_src == w_in.shape[2]

    w_in_p = _pack_columns(w_in.astype(BF16), cols)
    wq = mla_w_qb.reshape(depth, MLA_Q_RANK, g, MLA_NOPE + MLA_ROPE)
    wq = jnp.pad(wq, ((0, 0), (0, 0), (0, 0), (0, MLA_QK_PAD - MLA_NOPE - MLA_ROPE)))
    wq = wq.reshape(depth, MLA_Q_RANK, g * MLA_QK_PAD).astype(BF16)
    wkv = mla_w_kvb.reshape(depth, MLA_KV_RANK, g, MLA_NOPE + MLA_DV)
    wk = wkv[..., :MLA_NOPE].reshape(depth, MLA_KV_RANK, g * MLA_NOPE).astype(BF16)
    wv = wkv[..., MLA_NOPE:].reshape(depth, MLA_KV_RANK, g * MLA_DV).astype(BF16)
    w_out_b, w_ff1_b, w_ff2_b = w_out.astype(BF16), w_ff1.astype(BF16), w_ff2.astype(BF16)
    gqa_tab = _rope_tables(seq, ctx_len, HEAD_DIM)
    mla_tab = _rope_tables(seq, ctx_len, MLA_ROPE)

    n_rows = -(-(bsz + 1) // 8) * 8
    c_rows = jnp.concatenate([c, c_ctx[None, :], jnp.zeros((n_rows - bsz - 1, d), F32)], axis=0)
    mod_all = _mod_all(c_rows, w_mod, b_mod).reshape(depth, n_rows, 6, d)
    mod_all = jnp.stack([jnp.broadcast_to(mod_all[:, bsz:bsz + 1], (depth, bsz, 6, d)), mod_all[:, :bsz]], axis=2)

    h = jnp.concatenate([ctx, x], axis=1)
    for l in range(depth):
        mod, g4 = mod_all[l], g_norm[l]
        u = _inproj(h, mod, g4, w_in_p[l], ctx_len)
        qkv = _gdn_prep(u, gdn_conv[l], g, ctx_len, off)
        oa = _gdn_scan(qkv, u, _lane_row(gdn_a_log[l]), _lane_row(gdn_dt_bias[l]), g, ctx_len, off)
        qn, kn, vn = _gqa_prep(u, gqa_tab, gqa_qk_norm[l], g, off)
        ob = _attention(qn, kn, vn, g, g // 2, ctx_len)
        oc = _mlstm_scan(u, _lane_row(mlstm_gate_bias[l]), g, ctx_len, off)
        qm, km, vm = _mla_prep(u, mla_tab, mla_q_norm[l].reshape(1, -1), mla_kv_norm[l].reshape(1, -1),
                               wq[l], wk[l], wv[l], g, off)
        od = _attention(qm, km, vm, g, g, ctx_len)
        h = _outproj(oa, ob, oc, od, u, h, mod, g4, jnp.tile(gdn_norm[l], g).reshape(1, -1),
                     mlstm_norm[l].reshape(1, -1), w_out_b[l], g, ctx_len, off)
        h = _ffn(h, mod, g4, w_ff1_b[l], w_ff2_b[l], ctx_len)
    return h[:, ctx_len:, :]
```

```python
import functools
import math

import jax
import jax.numpy as jnp
from jax import lax
from jax.experimental import pallas as pl
from jax.experimental.pallas import tpu as pltpu

F32 = jnp.float32
BF16 = jnp.bfloat16

LANE = 128
HEAD_DIM = 128
CHUNK = 64
GRID_W = 64
ROPE_BASE = 10000.0
NORM_EPS = 1e-6
ML_DQK = 64
MLA_Q_RANK = 384
MLA_KV_RANK = 128
MLA_NOPE = 128
MLA_ROPE = 64
MLA_DV = 128
MLA_QK_PAD = 256
VMEM_LIMIT = 56 * 1024 * 1024
LOG2_E = math.log2(math.e)


def _dot(a, b):
    return jnp.dot(a.astype(BF16), b.astype(BF16), preferred_element_type=F32)


def _dot_nt(a, b):
    return lax.dot_general(a.astype(BF16), b.astype(BF16), (((1,), (1,)), ((), ())),
                           preferred_element_type=F32)


def _dot_tn(a, b):
    return lax.dot_general(a.astype(BF16), b.astype(BF16), (((0,), (0,)), ((), ())),
                           preferred_element_type=F32)


def _sigmoid(x):
    return 1.0 / (1.0 + jnp.exp(-x))


def _softplus(x):
    return jnp.maximum(x, 0.0) + jnp.log(1.0 + jnp.exp(-jnp.abs(x)))


def _rms(x, w):
    return x * lax.rsqrt(jnp.mean(x * x, axis=-1, keepdims=True) + NORM_EPS) * w


def _params(sem, vmem=None):
    return pltpu.CompilerParams(dimension_semantics=sem, vmem_limit_bytes=vmem)


def _layout(g):
    gw = g * HEAD_DIM
    order = [("mla_cqkv", MLA_Q_RANK + MLA_KV_RANK),
             ("gdn_q", gw), ("gdn_k", gw), ("gdn_v", gw), ("gdn_gate", gw),
             ("ml_q", gw), ("ml_k", gw), ("ml_v", gw), ("ml_og", gw), ("gqa_q", gw),
             ("gqa_k", gw // 2), ("gqa_v", gw // 2),
             ("gdn_ab", LANE), ("ml_if", LANE), ("mla_kpe", LANE)]
    off, pos = {}, 0
    for name, width in order:
        assert pos % width == 0, (name, pos, width)
        off[name] = pos
        pos += width
    return off, pos


def _packed_columns(g):
    gw = g * HEAD_DIM
    off, total = _layout(g)
    n_pad = -(-total // 512) * 512
    idx = [-1] * n_pad

    def put(name, src0, n, dst0=0):
        for t in range(n):
            idx[off[name] + dst0 + t] = src0 + t

    p = 0
    put("gdn_q", p, gw); p += gw
    put("gdn_k", p, gw); p += gw
    put("gdn_v", p, gw); p += gw
    put("gdn_gate", p, gw); p += gw
    put("gdn_ab", p, 4 * g); p += 4 * g
    put("gqa_q", p, gw); p += gw
    put("gqa_k", p, gw // 2); p += gw // 2
    put("gqa_v", p, gw // 2); p += gw // 2
    for name in ("ml_q", "ml_k"):
        for h in range(g):
            put(name, p + h * ML_DQK, ML_DQK, h * HEAD_DIM)
        p += g * ML_DQK
    put("ml_v", p, gw); p += gw
    put("ml_og", p, gw); p += gw
    put("ml_if", p, 4 * g); p += 4 * g
    put("mla_cqkv", p, MLA_Q_RANK + MLA_KV_RANK); p += MLA_Q_RANK + MLA_KV_RANK
    put("mla_kpe", p, MLA_ROPE); p += MLA_ROPE
    return idx, n_pad, p


def _pack_columns(w, cols):
    parts, start = [], 0
    for i in range(1, len(cols) + 1):
        run_ends = i == len(cols) or (cols[i] != cols[i - 1] + 1 if cols[i - 1] >= 0 else cols[i] >= 0)
        if run_ends:
            if cols[start] >= 0:
                parts.append(w[..., cols[start]:cols[start] + i - start])
            else:
                parts.append(jnp.zeros(w.shape[:-1] + (i - start,), w.dtype))
            start = i
    return jnp.concatenate(parts, axis=-1)


def _pick_tile(n, candidates):
    for c in candidates:
        if n % c == 0:
            return c
    raise ValueError(f"no tile for {n} in {candidates}")


def _row_tile(t, limit=640):
    for c in range(limit - limit % 16, 15, -16):
        if t % c == 0:
            return c
    raise ValueError(f"no row tile for {t}")


def _mod_kernel(c_ref, w_ref, b_ref, o_ref):
    c = c_ref[...]
    o_ref[0] = _dot(c * _sigmoid(c), w_ref[0]) + b_ref[0]


def _mod_all(c_rows, w_mod, b_mod):
    depth, d, n = w_mod.shape
    r = c_rows.shape[0]
    tn = _pick_tile(n, (1024, 512, 256, 128))
    return pl.pallas_call(
        _mod_kernel,
        name="mod",
        grid=(depth, n // tn),
        in_specs=[pl.BlockSpec((r, d), lambda l, j: (0, 0)),
                  pl.BlockSpec((1, d, tn), lambda l, j: (l, 0, j)),
                  pl.BlockSpec((1, 1, tn), lambda l, j: (l, 0, j))],
        out_specs=pl.BlockSpec((1, r, tn), lambda l, j: (l, 0, j)),
        out_shape=jax.ShapeDtypeStruct((depth, r, n), F32),
        compiler_params=_params(("parallel", "parallel"), VMEM_LIMIT),
    )(c_rows, w_mod, b_mod.reshape(depth, 1, n))


def _mod_rows(mod_ref, idx, row, ctx_len):
    if row is None:
        return mod_ref[0, 1, idx:idx + 1, :]
    return jnp.where(row < ctx_len, mod_ref[0, 0, idx:idx + 1, :], mod_ref[0, 1, idx:idx + 1, :])


def _unit_rms(x):
    return x * lax.rsqrt(jnp.mean(x * x, axis=-1, keepdims=True) + NORM_EPS)


def _per_tile_kind(tm, ctx_len, body, also=True):
    first = pl.program_id(1) * tm

    @pl.when(jnp.logical_and(also, first >= ctx_len))
    def _():
        body(None)

    @pl.when(jnp.logical_and(also, first < ctx_len))
    def _():
        body(first + lax.broadcasted_iota(jnp.int32, (tm, 1), 0))


def _prenorm_kernel(h_ref, mod_ref, g_ref, o_ref, *, ctx_len, tm):
    def body(row):
        gain = g_ref[0:1, :] * (1.0 + _mod_rows(mod_ref, 1, row, ctx_len))
        o_ref[0] = (_unit_rms(h_ref[0]) * gain + _mod_rows(mod_ref, 0, row, ctx_len)).astype(BF16)

    _per_tile_kind(tm, ctx_len, body)


def _inproj_kernel(x_ref, w_ref, o_ref):
    o_ref[0] = jnp.dot(x_ref[0], w_ref[...], preferred_element_type=F32)


def _inproj(h, mod, g4, w, ctx_len):
    b, t, d = h.shape
    n = w.shape[1]
    tm = _row_tile(t)
    xn = pl.pallas_call(
        functools.partial(_prenorm_kernel, ctx_len=ctx_len, tm=tm),
        name="prenorm",
        grid=(b, t // tm),
        in_specs=[pl.BlockSpec((1, tm, d), lambda bi, i: (bi, i, 0)),
                  pl.BlockSpec((1, 2, 6, d), lambda bi, i: (bi, 0, 0, 0)),
                  pl.BlockSpec((4, d), lambda bi, i: (0, 0))],
        out_specs=pl.BlockSpec((1, tm, d), lambda bi, i: (bi, i, 0)),
        out_shape=jax.ShapeDtypeStruct((b, t, d), BF16),
        compiler_params=_params(("parallel", "parallel"), VMEM_LIMIT),
    )(h, mod, g4)
    tm = _row_tile(t, 1280)
    tn = _pick_tile(n, (1536, 1024, 512))
    return pl.pallas_call(
        _inproj_kernel,
        name="inproj",
        grid=(n // tn, b, t // tm),
        in_specs=[pl.BlockSpec((1, tm, d), lambda j, bi, i: (bi, i, 0)),
                  pl.BlockSpec((d, tn), lambda j, bi, i: (0, j))],
        out_specs=pl.BlockSpec((1, tm, tn), lambda j, bi, i: (bi, i, j)),
        out_shape=jax.ShapeDtypeStruct((b, t, n), F32),
        compiler_params=_params(("parallel", "parallel", "parallel"), VMEM_LIMIT),
    )(xn, w)


def _gdn_prep_kernel(u_ref, w_ref, o_ref, *, g, ctx_len):
    j = pl.program_id(1)
    x = u_ref[0]
    t = x.shape[0]
    row = lax.broadcasted_iota(jnp.int32, (t, 1), 0)
    first = (row == 0) | (row == ctx_len)
    last = (row == ctx_len - 1) | (row == t - 1)
    x_prev = jnp.where(first, 0.0, pltpu.roll(x, 1, 0))
    x_next = jnp.where(last, 0.0, pltpu.roll(x, t - 1, 0))
    y = x_prev * w_ref[0:1, :] + x * w_ref[1:2, :] + x_next * w_ref[2:3, :]
    y = y * _sigmoid(y)
    yn = y * lax.rsqrt(jnp.sum(y * y, axis=-1, keepdims=True) + NORM_EPS)
    scale = jnp.where(j < g, HEAD_DIM ** -0.5, 1.0)
    o_ref[0] = jnp.where(j < 2 * g, yn * scale, y)


def _gdn_prep(u, conv_w, g, ctx_len, off):
    b, t, _ = u.shape
    blk0 = off["gdn_q"] // LANE
    return pl.pallas_call(
        functools.partial(_gdn_prep_kernel, g=g, ctx_len=ctx_len),
        name="gdn_prep",
        grid=(b, 3 * g),
        in_specs=[pl.BlockSpec((1, t, LANE), lambda bi, j: (bi, 0, blk0 + j)),
                  pl.BlockSpec((3, LANE), lambda bi, j: (0, j))],
        out_specs=pl.BlockSpec((1, t, LANE), lambda bi, j: (bi, 0, j)),
        out_shape=jax.ShapeDtypeStruct((b, t, 3 * g * LANE), F32),
        compiler_params=_params(("parallel", "parallel"), VMEM_LIMIT),
    )(u, conv_w)


def _chunk_masks(fwd):
    ri = lax.broadcasted_iota(jnp.int32, (CHUNK, CHUNK), 0)
    ci = lax.broadcasted_iota(jnp.int32, (CHUNK, CHUNK), 1)
    if fwd:
        return ri >= ci, ri > ci, ri == ci
    return ri <= ci, ri < ci, ri == ci


def _bcast(col):
    return jnp.broadcast_to(col, (CHUNK, LANE))


def _split3(x):
    x1 = x.astype(BF16)
    r = x - x1.astype(F32)
    x2 = r.astype(BF16)
    return x1, x2, (r - x2.astype(F32)).astype(BF16)


def _chunk_cumsum(x, incl):
    lmat = jnp.where(incl, 1.0, 0.0).astype(BF16)
    p1, p2, p3 = _split3(x)
    return (jnp.dot(lmat, p1, preferred_element_type=F32) + jnp.dot(lmat, p2, preferred_element_type=F32)
            + jnp.dot(lmat, p3, preferred_element_type=F32))


def _pair_diff(g_b):
    return g_b[:, :CHUNK] - g_b.T[:CHUNK, :]


def _tri_inverse_all(a_list, diag):
    ri = lax.broadcasted_iota(jnp.int32, (CHUNK, CHUNK), 0)
    ci = lax.broadcasted_iota(jnp.int32, (CHUNK, CHUNK), 1)
    eye = diag.astype(F32)
    ts = [jnp.where((ri >> 1) == (ci >> 1), eye - a, 0.0) for a in a_list]
    for lb in range(1, int(math.log2(CHUNK))):
        join = ((ri >> (lb + 1)) == (ci >> (lb + 1))) & ((ri >> lb) != (ci >> lb))
        us = [_dot(jnp.where(join, a, 0.0), t) for a, t in zip(a_list, ts)]
        ts = [t - _dot(t, u) for t, u in zip(ts, us)]
    return ts


def _gdn_scan_kernel(qf_ref, gf_ref, qb_ref, gb_ref, al_ref, dt_ref, of_ref, ob_ref, s_ref, *, g, nb):
    @pl.when(pl.program_id(1) == 0)
    def _():
        s_ref[...] = jnp.zeros_like(s_ref)

    gw = g * HEAD_DIM
    streams = range(nb * 2 * g)
    fwd, incl, strict, q, k, v, g_b, beta_b, dest = [], [], [], [], [], [], [], [], []
    diag = _chunk_masks(True)[2]
    for bb in range(nb):
        for d, (q_ref, gt_ref) in enumerate(((qf_ref, gf_ref), (qb_ref, gb_ref))):
            m_incl, m_strict, _ = _chunk_masks(d == 0)
            gates = gt_ref[bb]
            la = -jnp.exp(al_ref[...]) * _softplus(gates + dt_ref[...])
            g_all = _chunk_cumsum(la, m_incl)
            be = _sigmoid(gates)
            for h in range(g):
                c = d * g + h
                fwd.append(d == 0)
                incl.append(m_incl)
                strict.append(m_strict)
                dest.append((of_ref if d == 0 else ob_ref, bb, h))
                q.append(q_ref[bb, :, h * HEAD_DIM:(h + 1) * HEAD_DIM])
                k.append(q_ref[bb, :, gw + h * HEAD_DIM:gw + (h + 1) * HEAD_DIM])
                v.append(q_ref[bb, :, 2 * gw + h * HEAD_DIM:2 * gw + (h + 1) * HEAD_DIM])
                g_b.append(_bcast(g_all[:, c:c + 1]))
                beta_b.append(_bcast(be[:, 2 * g + c:2 * g + c + 1]))

    decay = [jnp.exp(jnp.where(incl[i], _pair_diff(g_b[i]), -jnp.inf)) for i in streams]
    kb = [k[i] * beta_b[i] for i in streams]
    kq = [_dot_nt(jnp.concatenate([kb[i], q[i]], axis=0), k[i]) for i in streams]
    a = [jnp.where(strict[i], kq[i][:CHUNK] * decay[i], 0.0) for i in streams]
    attn = [kq[i][CHUNK:] * decay[i] for i in streams]
    t = _tri_inverse_all(a, diag)
    eg = [jnp.exp(g_b[i]) for i in streams]
    vw = [_dot(t[i], jnp.concatenate([v[i] * beta_b[i], kb[i] * eg[i]], axis=1)) for i in streams]
    s = [s_ref[i] for i in streams]
    ws = [_dot(jnp.concatenate([vw[i][:, HEAD_DIM:], q[i] * eg[i]], axis=0), s[i]) for i in streams]
    v_new = [vw[i][:, :HEAD_DIM] - ws[i][:CHUNK] for i in streams]
    o = [ws[i][CHUNK:] + _dot(attn[i], v_new[i]) for i in streams]
    g_end = [g_b[i][CHUNK - 1:CHUNK, :] if fwd[i] else g_b[i][0:1, :] for i in streams]
    s_new = [s[i] * jnp.exp(g_end[i]) + _dot_tn(k[i] * jnp.exp(g_end[i] - g_b[i]), v_new[i]) for i in streams]
    for i in streams:
        o_ref, bb, h = dest[i]
        o_ref[bb, :, h * HEAD_DIM:(h + 1) * HEAD_DIM] = o[i]
        s_ref[i] = s_new[i]


SCAN_BATCH = 4


def _scan_batch(b):
    return math.gcd(b, SCAN_BATCH)


def _scan_chunk_maps(n_ctx, n_all):
    def fwd(bi, s):
        return (bi, s, 0)

    def bwd(bi, s):
        return (bi, jnp.where(s < n_ctx, n_ctx - 1 - s, n_all + n_ctx - 1 - s), 0)

    return fwd, bwd


def _gdn_scan(qkv, u, a_log_row, dt_row, g, ctx_len, off):
    b, t, _ = qkv.shape
    n_all, n_ctx = t // CHUNK, ctx_len // CHUNK
    fwd, bwd = _scan_chunk_maps(n_ctx, n_all)
    gblk = off["gdn_ab"] // LANE

    def gate_map(m):
        return lambda bi, s: (m(bi, s)[0], m(bi, s)[1], gblk)

    gw = g * HEAD_DIM
    nb = _scan_batch(b)
    return pl.pallas_call(
        functools.partial(_gdn_scan_kernel, g=g, nb=nb),
        name="gdn_scan",
        grid=(b // nb, n_all),
        in_specs=[pl.BlockSpec((nb, CHUNK, 3 * gw), fwd),
                  pl.BlockSpec((nb, CHUNK, LANE), gate_map(fwd)),
                  pl.BlockSpec((nb, CHUNK, 3 * gw), bwd),
                  pl.BlockSpec((nb, CHUNK, LANE), gate_map(bwd)),
                  pl.BlockSpec((1, LANE), lambda bi, s: (0, 0)),
                  pl.BlockSpec((1, LANE), lambda bi, s: (0, 0))],
        out_specs=[pl.BlockSpec((nb, CHUNK, gw), fwd), pl.BlockSpec((nb, CHUNK, gw), bwd)],
        out_shape=[jax.ShapeDtypeStruct((b, t, gw), F32)] * 2,
        scratch_shapes=[pltpu.VMEM((nb * 2 * g, HEAD_DIM, HEAD_DIM), F32)],
        compiler_params=_params(("parallel", "arbitrary")),
    )(qkv, u, qkv, u, a_log_row, dt_row)


def _mlstm_scan_kernel(qf_ref, kf_ref, vf_ref, gf_ref, qb_ref, kb_ref, vb_ref, gb_ref, bias_ref,
                       of_ref, ob_ref, c_ref, n_ref, m_ref, *, g, nb):
    @pl.when(pl.program_id(1) == 0)
    def _():
        c_ref[...] = jnp.zeros_like(c_ref)
        n_ref[...] = jnp.zeros_like(n_ref)
        m_ref[...] = jnp.zeros_like(m_ref)

    streams = range(nb * 2 * g)
    fwd, incl, q, k, v, b_b, ip_b, dest = [], [], [], [], [], [], [], []
    dirs = ((qf_ref, kf_ref, vf_ref, gf_ref), (qb_ref, kb_ref, vb_ref, gb_ref))
    for bb in range(nb):
        for d, (q_ref, k_ref, v_ref, gt_ref) in enumerate(dirs):
            m_incl = _chunk_masks(d == 0)[0]
            gates = gt_ref[bb] + bias_ref[...]
            logsig = jnp.minimum(gates, 0.0) - jnp.log(1.0 + jnp.exp(-jnp.abs(gates)))
            b_all = _chunk_cumsum(logsig, m_incl)
            for h in range(g):
                c = d * g + h
                sl = slice(h * HEAD_DIM, (h + 1) * HEAD_DIM)
                fwd.append(d == 0)
                incl.append(m_incl)
                dest.append((of_ref if d == 0 else ob_ref, bb, h))
                q.append(q_ref[bb, :, sl] * (ML_DQK ** -0.5))
                k.append(k_ref[bb, :, sl])
                v.append(v_ref[bb, :, sl])
                b_b.append(_bcast(b_all[:, 2 * g + c:2 * g + c + 1]))
                ip_b.append(_bcast(gates[:, c:c + 1]))

    dmat = [jnp.where(incl[i], _pair_diff(b_b[i]) + ip_b[i].T[:CHUNK, :], -jnp.inf) for i in streams]
    qk = [_dot_nt(q[i], k[i]) for i in streams]
    cm = [c_ref[i] for i in streams]
    nrow = [n_ref[i] for i in streams]
    mrow = [m_ref[i] for i in streams]
    b_end = [b_b[i][CHUNK - 1:CHUNK, :] if fwd[i] else b_b[i][0:1, :] for i in streams]
    w_b = [b_end[i] - b_b[i] + ip_b[i] for i in streams]
    inter = [b_b[i] + mrow[i] for i in streams]
    m_t = [jnp.maximum(inter[i], jnp.max(dmat[i], axis=1, keepdims=True)) for i in streams]
    s_inter = [jnp.exp(inter[i] - m_t[i]) for i in streams]
    p = [jnp.exp(dmat[i] - m_t[i][:, :CHUNK]) * qk[i] for i in streams]
    num = [s_inter[i] * _dot(q[i], cm[i]) + _dot(p[i], v[i]) for i in streams]
    den = [s_inter[i] * jnp.sum(q[i] * nrow[i], axis=1, keepdims=True) + jnp.sum(p[i], axis=1, keepdims=True)
           for i in streams]
    hh = [num[i] / jnp.maximum(jnp.abs(den[i]), jnp.exp(-m_t[i])) for i in streams]
    m_new = [jnp.maximum(b_end[i] + mrow[i], jnp.max(w_b[i], axis=0, keepdims=True)) for i in streams]
    s_old = [jnp.exp(b_end[i] + mrow[i] - m_new[i]) for i in streams]
    wk = [jnp.exp(w_b[i] - m_new[i]) * k[i] for i in streams]
    cm_new = [s_old[i] * cm[i] + _dot_tn(wk[i], v[i]) for i in streams]
    n_new = [s_old[i] * nrow[i] + jnp.sum(wk[i], axis=0, keepdims=True) for i in streams]
    for i in streams:
        o_ref, bb, h = dest[i]
        o_ref[bb, :, h * HEAD_DIM:(h + 1) * HEAD_DIM] = hh[i]
        c_ref[i] = cm_new[i]
        n_ref[i] = n_new[i]
        m_ref[i] = m_new[i]


def _mlstm_scan(u, bias_row, g, ctx_len, off):
    b, t, _ = u.shape
    n_all, n_ctx = t // CHUNK, ctx_len // CHUNK
    fwd, bwd = _scan_chunk_maps(n_ctx, n_all)
    gw = g * HEAD_DIM

    def col_map(m, blk):
        return lambda bi, s: (m(bi, s)[0], m(bi, s)[1], blk)

    nb = _scan_batch(b)

    def specs(m):
        return [pl.BlockSpec((nb, CHUNK, gw), col_map(m, off["ml_q"] // gw)),
                pl.BlockSpec((nb, CHUNK, gw), col_map(m, off["ml_k"] // gw)),
                pl.BlockSpec((nb, CHUNK, gw), col_map(m, off["ml_v"] // gw)),
                pl.BlockSpec((nb, CHUNK, LANE), col_map(m, off["ml_if"] // LANE))]

    return pl.pallas_call(
        functools.partial(_mlstm_scan_kernel, g=g, nb=nb),
        name="mlstm_scan",
        grid=(b // nb, n_all),
        in_specs=specs(fwd) + specs(bwd) + [pl.BlockSpec((1, LANE), lambda bi, s: (0, 0))],
        out_specs=[pl.BlockSpec((nb, CHUNK, gw), fwd), pl.BlockSpec((nb, CHUNK, gw), bwd)],
        out_shape=[jax.ShapeDtypeStruct((b, t, gw), F32)] * 2,
        scratch_shapes=[pltpu.VMEM((nb * 2 * g, HEAD_DIM, HEAD_DIM), F32),
                        pltpu.VMEM((nb * 2 * g, 1, HEAD_DIM), F32),
                        pltpu.VMEM((nb * 2 * g, 1, HEAD_DIM), F32)],
        compiler_params=_params(("parallel", "arbitrary")),
    )(u, u, u, u, u, u, u, u, bias_row)


def _rope(x, c, s_up, s_dn, shift):
    return x * c + pltpu.roll(x, LANE - shift, 1) * s_up + pltpu.roll(x, shift, 1) * s_dn


def _gqa_prep_kernel(q_ref, k_ref, v_ref, c_ref, su_ref, sd_ref, n_ref, qo_ref, ko_ref, vo_ref, *, g):
    c, su, sd = c_ref[...], su_ref[...], sd_ref[...]
    scale = HEAD_DIM ** -0.5 * LOG2_E
    for h in range(g):
        sl = slice(h * HEAD_DIM, (h + 1) * HEAD_DIM)
        qo_ref[0, :, sl] = (_rope(_rms(q_ref[0, :, sl], n_ref[0:1, :]), c, su, sd, 32) * scale).astype(BF16)
    for h in range(g // 2):
        sl = slice(h * HEAD_DIM, (h + 1) * HEAD_DIM)
        ko_ref[0, :, sl] = _rope(_rms(k_ref[0, :, sl], n_ref[1:2, :]), c, su, sd, 32).astype(BF16)
    vo_ref[0] = v_ref[0].astype(BF16)


def _gqa_prep(u, tables, qk_norm, g, off):
    b, t, _ = u.shape
    tm = _row_tile(t)
    gw, kw = g * HEAD_DIM, g * HEAD_DIM // 2
    tab = pl.BlockSpec((tm, LANE), lambda bi, i: (i, 0))
    return pl.pallas_call(
        functools.partial(_gqa_prep_kernel, g=g),
        name="gqa_prep",
        grid=(b, t // tm),
        in_specs=[pl.BlockSpec((1, tm, gw), lambda bi, i: (bi, i, off["gqa_q"] // gw)),
                  pl.BlockSpec((1, tm, kw), lambda bi, i: (bi, i, off["gqa_k"] // kw)),
                  pl.BlockSpec((1, tm, kw), lambda bi, i: (bi, i, off["gqa_v"] // kw)),
                  tab, tab, tab,
                  pl.BlockSpec((2, HEAD_DIM), lambda bi, i: (0, 0))],
        out_specs=[pl.BlockSpec((1, tm, gw), lambda bi, i: (bi, i, 0)),
                   pl.BlockSpec((1, tm, kw), lambda bi, i: (bi, i, 0)),
                   pl.BlockSpec((1, tm, kw), lambda bi, i: (bi, i, 0))],
        out_shape=[jax.ShapeDtypeStruct((b, t, gw), BF16),
                   jax.ShapeDtypeStruct((b, t, kw), BF16),
                   jax.ShapeDtypeStruct((b, t, kw), BF16)],
        compiler_params=_params(("parallel", "parallel")),
    )(u, u, u, *tables, qk_norm)


def _mla_prep_kernel(x_ref, kpe_ref, c_ref, su_ref, sd_ref, qn_ref, kvn_ref, wq_ref, wk_ref, wv_ref,
                     qo_ref, ko_ref, vo_ref, *, g):
    c, su, sd = c_ref[...], su_ref[...], sd_ref[...]
    scale = (MLA_NOPE + MLA_ROPE) ** -0.5 * LOG2_E
    x = x_ref[0]
    cq = _rms(x[:, :MLA_Q_RANK], qn_ref[...])
    ckv = _rms(x[:, MLA_Q_RANK:], kvn_ref[...])
    q = _dot(cq, wq_ref[...])
    kn = _dot(ckv, wk_ref[...])
    vo_ref[0] = _dot(ckv, wv_ref[...]).astype(BF16)
    kpe = _rope(kpe_ref[0], c, su, sd, 16).astype(BF16)
    for h in range(g):
        lo = h * MLA_QK_PAD
        qo_ref[0, :, lo:lo + LANE] = (q[:, lo:lo + LANE] * scale).astype(BF16)
        qo_ref[0, :, lo + LANE:lo + 2 * LANE] = (
            _rope(q[:, lo + LANE:lo + 2 * LANE], c, su, sd, 16) * scale).astype(BF16)
        ko_ref[0, :, lo:lo + LANE] = kn[:, h * LANE:(h + 1) * LANE].astype(BF16)
        ko_ref[0, :, lo + LANE:lo + 2 * LANE] = kpe


def _mla_prep(u, tables, q_norm, kv_norm, wq, wk, wv, g, off):
    b, t, _ = u.shape
    tm = _row_tile(t)
    cw = MLA_Q_RANK + MLA_KV_RANK
    tab = pl.BlockSpec((tm, LANE), lambda bi, i: (i, 0))

    def full(a):
        return pl.BlockSpec(a.shape, lambda bi, i: (0,) * a.ndim)

    return pl.pallas_call(
        functools.partial(_mla_prep_kernel, g=g),
        name="mla_prep",
        grid=(b, t // tm),
        in_specs=[pl.BlockSpec((1, tm, cw), lambda bi, i: (bi, i, off["mla_cqkv"] // cw)),
                  pl.BlockSpec((1, tm, LANE), lambda bi, i: (bi, i, off["mla_kpe"] // LANE)),
                  tab, tab, tab, full(q_norm), full(kv_norm), full(wq), full(wk), full(wv)],
        out_specs=[pl.BlockSpec((1, tm, g * MLA_QK_PAD), lambda bi, i: (bi, i, 0)),
                   pl.BlockSpec((1, tm, g * MLA_QK_PAD), lambda bi, i: (bi, i, 0)),
                   pl.BlockSpec((1, tm, g * MLA_DV), lambda bi, i: (bi, i, 0))],
        out_shape=[jax.ShapeDtypeStruct((b, t, g * MLA_QK_PAD), BF16),
                   jax.ShapeDtypeStruct((b, t, g * MLA_QK_PAD), BF16),
                   jax.ShapeDtypeStruct((b, t, g * MLA_DV), BF16)],
        compiler_params=_params(("parallel", "parallel")),
    )(u, u, *tables, q_norm, kv_norm, wq, wk, wv)


KV_HEADS_PER_STEP = 2


def _attn_kernel(q_ref, k_ref, v_ref, o_ref, *, ctx_len, tq, hp, rep, dq, dv):
    heads = range(hp)

    def attend(nk):
        s = [lax.dot_general(q_ref[0, :, j * dq:(j + 1) * dq], k_ref[0, :nk, (j // rep) * dq:(j // rep + 1) * dq],
                             (((1,), (1,)), ((), ())), preferred_element_type=F32) for j in heads]
        p = [jnp.exp2(s[j] - jnp.max(s[j], axis=-1, keepdims=True)) for j in heads]
        l = [jnp.sum(p[j], axis=-1, keepdims=True) for j in heads]
        o = [jnp.dot(p[j].astype(BF16), v_ref[0, :nk, (j // rep) * dv:(j // rep + 1) * dv],
                     preferred_element_type=F32) for j in heads]
        for j in heads:
            o_ref[0, :, j * dv:(j + 1) * dv] = (o[j] / l[j]).astype(o_ref.dtype)

    is_ctx = pl.program_id(2) * tq < ctx_len

    @pl.when(is_ctx)
    def _():
        attend(ctx_len)

    @pl.when(jnp.logical_not(is_ctx))
    def _():
        attend(k_ref.shape[1])


def _attention(q, k, v, heads, kv_heads, ctx_len):
    b, t, _ = q.shape
    dq, dv = q.shape[2] // heads, v.shape[2] // kv_heads
    rep = heads // kv_heads
    kvp = math.gcd(kv_heads, KV_HEADS_PER_STEP)
    hp = kvp * rep
    tq = math.gcd(256, ctx_len)
    return pl.pallas_call(
        functools.partial(_attn_kernel, ctx_len=ctx_len, tq=tq, hp=hp, rep=rep, dq=dq, dv=dv),
        name="attn",
        grid=(b, heads // hp, t // tq),
        in_specs=[pl.BlockSpec((1, tq, hp * dq), lambda bi, h, i: (bi, i, h)),
                  pl.BlockSpec((1, t, kvp * dq), lambda bi, h, i: (bi, 0, h * hp // (rep * kvp))),
                  pl.BlockSpec((1, t, kvp * dv), lambda bi, h, i: (bi, 0, h * hp // (rep * kvp)))],
        out_specs=pl.BlockSpec((1, tq, hp * dv), lambda bi, h, i: (bi, i, h)),
        out_shape=jax.ShapeDtypeStruct((b, t, heads * dv), BF16),
        compiler_params=_params(("parallel", "parallel", "parallel"), VMEM_LIMIT),
    )(q, k, v)


def _head_norm(x, w_row, g):
    parts = []
    for h in range(g):
        xs = x[:, h * HEAD_DIM:(h + 1) * HEAD_DIM]
        parts.append(xs * lax.rsqrt(jnp.mean(xs * xs, axis=-1, keepdims=True) + NORM_EPS))
    return jnp.concatenate(parts, axis=-1) * w_row


def _outproj_kernel(af_ref, ab_ref, ga_ref, yb_ref, cf_ref, cb_ref, gc_ref, yd_ref, h_ref, mod_ref, g_ref,
                    na_ref, nc_ref, w_ref, o_ref, *, g, ctx_len, tm):
    ga = ga_ref[0]
    ya = _head_norm(af_ref[0] + ab_ref[0], na_ref[...], g) * (ga * _sigmoid(ga))
    yc = _head_norm(cf_ref[0] + cb_ref[0], nc_ref[...], g) * _sigmoid(gc_ref[0])
    y = jnp.concatenate([ya.astype(BF16), yb_ref[0], yc.astype(BF16), yd_ref[0]], axis=-1)
    z = jnp.dot(y, w_ref[...], preferred_element_type=F32)

    def residual(row):
        gain = g_ref[1:2, :] * _mod_rows(mod_ref, 2, row, ctx_len)
        o_ref[0] = h_ref[0] + _unit_rms(z) * gain

    _per_tile_kind(tm, ctx_len, residual)


def _outproj(oa, ob, oc, od, u, h, mod, g4, na_row, nc_row, w, g, ctx_len, off):
    b, t, d = h.shape
    tm = _row_tile(t)
    gw = g * HEAD_DIM

    def rows(width, blk=0):
        return pl.BlockSpec((1, tm, width), lambda bi, i: (bi, i, blk))

    return pl.pallas_call(
        functools.partial(_outproj_kernel, g=g, ctx_len=ctx_len, tm=tm),
        name="outproj",
        grid=(b, t // tm),
        in_specs=[rows(gw), rows(gw), rows(gw, off["gdn_gate"] // gw), rows(gw),
                  rows(gw), rows(gw), rows(gw, off["ml_og"] // gw), rows(gw),
                  rows(d),
                  pl.BlockSpec((1, 2, 6, d), lambda bi, i: (bi, 0, 0, 0)),
                  pl.BlockSpec((4, d), lambda bi, i: (0, 0)),
                  pl.BlockSpec((1, gw), lambda bi, i: (0, 0)),
                  pl.BlockSpec((1, gw), lambda bi, i: (0, 0)),
                  pl.BlockSpec(w.shape, lambda bi, i: (0, 0))],
        out_specs=rows(d),
        out_shape=jax.ShapeDtypeStruct((b, t, d), F32),
        compiler_params=_params(("parallel", "parallel"), VMEM_LIMIT),
    )(oa[0], oa[1], u, ob, oc[0], oc[1], u, od, h, mod, g4, na_row, nc_row, w)


FFN_ROW_PIECES = 2


def _ffn_kernel(h_ref, mod_ref, g_ref, w1_ref, w2_ref, o_ref, xn_ref, acc_ref, *, ctx_len, tm):
    k = pl.program_id(2)
    last = pl.num_programs(2) - 1
    pieces = [slice(lo, lo + tm // FFN_ROW_PIECES) for lo in range(0, tm, tm // FFN_ROW_PIECES)]

    def mlp(rows):
        hid = jnp.maximum(jnp.dot(xn_ref[rows, :], w1_ref[...], preferred_element_type=F32), 0.0)
        return jnp.dot((hid * hid).astype(BF16), w2_ref[...], preferred_element_type=F32)

    def piece_rows(row, rows):
        return None if row is None else row[rows]

    def first_step(row):
        for rows in pieces:
            r = piece_rows(row, rows)
            gain = g_ref[2:3, :] * (1.0 + _mod_rows(mod_ref, 4, r, ctx_len))
            xn_ref[rows, :] = (_unit_rms(h_ref[0, rows, :]) * gain + _mod_rows(mod_ref, 3, r, ctx_len)).astype(BF16)
            acc_ref[rows, :] = mlp(rows)

    def last_step(row):
        for rows in pieces:
            gain = g_ref[3:4, :] * _mod_rows(mod_ref, 5, piece_rows(row, rows), ctx_len)
            o_ref[0, rows, :] = h_ref[0, rows, :] + _unit_rms(acc_ref[rows, :] + mlp(rows)) * gain

    _per_tile_kind(tm, ctx_len, first_step, also=k == 0)

    @pl.when(jnp.logical_and(k > 0, k < last))
    def _():
        acc_ref[...] += mlp(slice(None))

    _per_tile_kind(tm, ctx_len, last_step, also=k == last)


def _ffn(h, mod, g4, w1, w2, ctx_len):
    b, t, d = h.shape
    ff = w1.shape[1]
    tm = _row_tile(t)
    tf = _pick_tile(ff, (1024, 512, 256, 128))
    assert ff // tf >= 2 and tm % (16 * FFN_ROW_PIECES) == 0
    return pl.pallas_call(
        functools.partial(_ffn_kernel, ctx_len=ctx_len, tm=tm),
        name="ffn",
        grid=(b, t // tm, ff // tf),
        in_specs=[pl.BlockSpec((1, tm, d), lambda bi, i, k: (bi, i, 0)),
                  pl.BlockSpec((1, 2, 6, d), lambda bi, i, k: (bi, 0, 0, 0)),
                  pl.BlockSpec((4, d), lambda bi, i, k: (0, 0)),
                  pl.BlockSpec((d, tf), lambda bi, i, k: (0, k)),
                  pl.BlockSpec((tf, d), lambda bi, i, k: (k, 0))],
        out_specs=pl.BlockSpec((1, tm, d), lambda bi, i, k: (bi, i, 0)),
        out_shape=jax.ShapeDtypeStruct((b, t, d), F32),
        scratch_shapes=[pltpu.VMEM((tm, d), BF16), pltpu.VMEM((tm, d), F32)],
        compiler_params=_params(("parallel", "parallel", "arbitrary"), VMEM_LIMIT),
    )(h, mod, g4, w1, w2)


def _rope_tables(seq, ctx_len, dim):
    quarter = dim // 4
    t = jnp.arange(seq, dtype=jnp.int32)
    inv_freq = ROPE_BASE ** (-jnp.arange(quarter, dtype=F32) / quarter)
    ang_r = (t // GRID_W).astype(F32)[:, None] * inv_freq
    ang_c = (t % GRID_W).astype(F32)[:, None] * inv_freq
    zero = jnp.zeros_like(ang_r)
    cos = jnp.concatenate([jnp.cos(ang_r)] * 2 + [jnp.cos(ang_c)] * 2, axis=-1)
    s_up = jnp.concatenate([-jnp.sin(ang_r), zero, -jnp.sin(ang_c), zero], axis=-1)
    s_dn = jnp.concatenate([zero, jnp.sin(ang_r), zero, jnp.sin(ang_c)], axis=-1)

    def finish(tab, fill):
        tab = jnp.pad(tab, ((0, 0), (0, LANE - dim)), constant_values=fill)
        return jnp.pad(tab, ((ctx_len, 0), (0, 0)), constant_values=fill)

    return finish(cos, 1.0), finish(s_up, 0.0), finish(s_dn, 0.0)


def _lane_row(v):
    v = v.reshape(1, -1)
    return jnp.pad(v, ((0, 0), (0, LANE - v.shape[1])))


def kernel(x, c, ctx, c_ctx, w_mod, b_mod, g_norm, w_in, gdn_conv, gdn_a_log, gdn_dt_bias, gdn_norm,
           gqa_qk_norm, mlstm_gate_bias, mlstm_norm, mla_q_norm, mla_kv_norm, mla_w_qb, mla_w_kvb,
           w_out, w_ff1, w_ff2):
    bsz, seq, d = x.shape
    ctx_len = ctx.shape[1]
    depth = w_in.shape[0]
    g = w_out.shape[1] // (4 * HEAD_DIM)
    assert ctx_len % CHUNK == 0 and seq % CHUNK == 0 and seq % GRID_W == 0
    off, _ = _layout(g)
    cols, n_pad, n_src = _packed_columns(g)
    assert n_src == w_in.shape[2]

    w_in_p = _pack_columns(w_in.astype(BF16), cols)
    wq = mla_w_qb.reshape(depth, MLA_Q_RANK, g, MLA_NOPE + MLA_ROPE)
    wq = jnp.pad(wq, ((0, 0), (0, 0), (0, 0), (0, MLA_QK_PAD - MLA_NOPE - MLA_ROPE)))
    wq = wq.reshape(depth, MLA_Q_RANK, g * MLA_QK_PAD).astype(BF16)
    wkv = mla_w_kvb.reshape(depth, MLA_KV_RANK, g, MLA_NOPE + MLA_DV)
    wk = wkv[..., :MLA_NOPE].reshape(depth, MLA_KV_RANK, g * MLA_NOPE).astype(BF16)
    wv = wkv[..., MLA_NOPE:].reshape(depth, MLA_KV_RANK, g * MLA_DV).astype(BF16)
    w_out_b, w_ff1_b, w_ff2_b = w_out.astype(BF16), w_ff1.astype(BF16), w_ff2.astype(BF16)
    gqa_tab = _rope_tables(seq, ctx_len, HEAD_DIM)
    mla_tab = _rope_tables(seq, ctx_len, MLA_ROPE)

    n_rows = -(-(bsz + 1) // 8) * 8
    c_rows = jnp.concatenate([c, c_ctx[None, :], jnp.zeros((n_rows - bsz - 1, d), F32)], axis=0)
    mod_all = _mod_all(c_rows, w_mod, b_mod).reshape(depth, n_rows, 6, d)
    mod_all = jnp.stack([jnp.broadcast_to(mod_all[:, bsz:bsz + 1], (depth, bsz, 6, d)), mod_all[:, :bsz]], axis=2)

    h = jnp.concatenate([ctx, x], axis=1)
    for l in range(depth):
        mod, g4 = mod_all[l], g_norm[l]
        u = _inproj(h, mod, g4, w_in_p[l], ctx_len)
        qkv = _gdn_prep(u, gdn_conv[l], g, ctx_len, off)
        oa = _gdn_scan(qkv, u, _lane_row(gdn_a_log[l]), _lane_row(gdn_dt_bias[l]), g, ctx_len, off)
        qn, kn, vn = _gqa_prep(u, gqa_tab, gqa_qk_norm[l], g, off)
        ob = _attention(qn, kn, vn, g, g // 2, ctx_len)
        oc = _mlstm_scan(u, _lane_row(mlstm_gate_bias[l]), g, ctx_len, off)
        qm, km, vm = _mla_prep(u, mla_tab, mla_q_norm[l].reshape(1, -1), mla_kv_norm[l].reshape(1, -1),
                               wq[l], wk[l], wv[l], g, off)
        od = _attention(qm, km, vm, g, g, ctx_len)
        h = _outproj(oa, ob, oc, od, u, h, mod, g4, jnp.tile(gdn_norm[l], g).reshape(1, -1),
                     mlstm_norm[l].reshape(1, -1), w_out_b[l], g, ctx_len, off)
        h = _ffn(h, mod, g4, w_ff1_b[l], w_ff2_b[l], ctx_len)
    return h[:, ctx_len:, :]
```

```python
import functools
import math

import jax
import jax.numpy as jnp
from jax import lax
from jax.experimental import pallas as pl
from jax.experimental.pallas import tpu as pltpu

F32 = jnp.float32
BF16 = jnp.bfloat16

LANE = 128
HEAD_DIM = 128
CHUNK = 64
GRID_W = 64
ROPE_BASE = 10000.0
NORM_EPS = 1e-6
ML_DQK = 64
MLA_Q_RANK = 384
MLA_KV_RANK = 128
MLA_NOPE = 128
MLA_ROPE = 64
MLA_DV = 128
MLA_QK_PAD = 256
VMEM_LIMIT = 56 * 1024 * 1024
LOG2_E = math.log2(math.e)


def _dot(a, b):
    return jnp.dot(a.astype(BF16), b.astype(BF16), preferred_element_type=F32)


def _dot_nt(a, b):
    return lax.dot_general(a.astype(BF16), b.astype(BF16), (((1,), (1,)), ((), ())),
                           preferred_element_type=F32)


def _dot_tn(a, b):
    return lax.dot_general(a.astype(BF16), b.astype(BF16), (((0,), (0,)), ((), ())),
                           preferred_element_type=F32)


def _sigmoid(x):
    return 1.0 / (1.0 + jnp.exp(-x))


def _softplus(x):
    return jnp.maximum(x, 0.0) + jnp.log(1.0 + jnp.exp(-jnp.abs(x)))


def _rms(x, w):
    return x * lax.rsqrt(jnp.mean(x * x, axis=-1, keepdims=True) + NORM_EPS) * w


def _params(sem, vmem=None):
    return pltpu.CompilerParams(dimension_semantics=sem, vmem_limit_bytes=vmem)


def _layout(g):
    gw = g * HEAD_DIM
    order = [("mla_cqkv", MLA_Q_RANK + MLA_KV_RANK),
             ("gdn_q", gw), ("gdn_k", gw), ("gdn_v", gw), ("gdn_gate", gw),
             ("ml_q", gw), ("ml_k", gw), ("ml_v", gw), ("ml_og", gw), ("gqa_q", gw),
             ("gqa_k", gw // 2), ("gqa_v", gw // 2),
             ("gdn_ab", LANE), ("ml_if", LANE), ("mla_kpe", LANE)]
    off, pos = {}, 0
    for name, width in order:
        assert pos % width == 0, (name, pos, width)
        off[name] = pos
        pos += width
    return off, pos


def _packed_columns(g):
    gw = g * HEAD_DIM
    off, total = _layout(g)
    n_pad = -(-total // 512) * 512
    idx = [-1] * n_pad

    def put(name, src0, n, dst0=0):
        for t in range(n):
            idx[off[name] + dst0 + t] = src0 + t

    p = 0
    put("gdn_q", p, gw); p += gw
    put("gdn_k", p, gw); p += gw
    put("gdn_v", p, gw); p += gw
    put("gdn_gate", p, gw); p += gw
    put("gdn_ab", p, 4 * g); p += 4 * g
    put("gqa_q", p, gw); p += gw
    put("gqa_k", p, gw // 2); p += gw // 2
    put("gqa_v", p, gw // 2); p += gw // 2
    for name in ("ml_q", "ml_k"):
        for h in range(g):
            put(name, p + h * ML_DQK, ML_DQK, h * HEAD_DIM)
        p += g * ML_DQK
    put("ml_v", p, gw); p += gw
    put("ml_og", p, gw); p += gw
    put("ml_if", p, 4 * g); p += 4 * g
    put("mla_cqkv", p, MLA_Q_RANK + MLA_KV_RANK); p += MLA_Q_RANK + MLA_KV_RANK
    put("mla_kpe", p, MLA_ROPE); p += MLA_ROPE
    return idx, n_pad, p


def _pack_columns(w, cols):
    parts, start = [], 0
    for i in range(1, len(cols) + 1):
        run_ends = i == len(cols) or (cols[i] != cols[i - 1] + 1 if cols[i - 1] >= 0 else cols[i] >= 0)
        if run_ends:
            if cols[start] >= 0:
                parts.append(w[..., cols[start]:cols[start] + i - start])
            else:
                parts.append(jnp.zeros(w.shape[:-1] + (i - start,), w.dtype))
            start = i
    return jnp.concatenate(parts, axis=-1)


def _pick_tile(n, candidates):
    for c in candidates:
        if n % c == 0:
            return c
    raise ValueError(f"no tile for {n} in {candidates}")


def _row_tile(t, limit=640):
    for c in range(limit - limit % 16, 15, -16):
        if t % c == 0:
            return c
    raise ValueError(f"no row tile for {t}")


def _mod_kernel(c_ref, w_ref, b_ref, o_ref):
    c = c_ref[...]
    o_ref[0] = _dot(c * _sigmoid(c), w_ref[0]) + b_ref[0]


def _mod_all(c_rows, w_mod, b_mod):
    depth, d, n = w_mod.shape
    r = c_rows.shape[0]
    tn = _pick_tile(n, (1024, 512, 256, 128))
    return pl.pallas_call(
        _mod_kernel,
        name="mod",
        grid=(depth, n // tn),
        in_specs=[pl.BlockSpec((r, d), lambda l, j: (0, 0)),
                  pl.BlockSpec((1, d, tn), lambda l, j: (l, 0, j)),
                  pl.BlockSpec((1, 1, tn), lambda l, j: (l, 0, j))],
        out_specs=pl.BlockSpec((1, r, tn), lambda l, j: (l, 0, j)),
        out_shape=jax.ShapeDtypeStruct((depth, r, n), F32),
        compiler_params=_params(("parallel", "parallel"), VMEM_LIMIT),
    )(c_rows, w_mod, b_mod.reshape(depth, 1, n))


def _mod_rows(mod_ref, idx, row, ctx_len):
    if row is None:
        return mod_ref[0, 1, idx:idx + 1, :]
    return jnp.where(row < ctx_len, mod_ref[0, 0, idx:idx + 1, :], mod_ref[0, 1, idx:idx + 1, :])


def _unit_rms(x):
    return x * lax.rsqrt(jnp.mean(x * x, axis=-1, keepdims=True) + NORM_EPS)


def _per_tile_kind(tm, ctx_len, body, also=True):
    first = pl.program_id(1) * tm

    @pl.when(jnp.logical_and(also, first >= ctx_len))
    def _():
        body(None)

    @pl.when(jnp.logical_and(also, first < ctx_len))
    def _():
        body(first + lax.broadcasted_iota(jnp.int32, (tm, 1), 0))


def _prenorm_kernel(h_ref, mod_ref, g_ref, o_ref, *, ctx_len, tm):
    def body(row):
        gain = g_ref[0:1, :] * (1.0 + _mod_rows(mod_ref, 1, row, ctx_len))
        o_ref[0] = (_unit_rms(h_ref[0]) * gain + _mod_rows(mod_ref, 0, row, ctx_len)).astype(BF16)

    _per_tile_kind(tm, ctx_len, body)


def _inproj_kernel(x_ref, w_ref, o_ref):
    o_ref[0] = jnp.dot(x_ref[0], w_ref[...], preferred_element_type=F32)


def _inproj(h, mod, g4, w, ctx_len):
    b, t, d = h.shape
    n = w.shape[1]
    tm = _row_tile(t)
    xn = pl.pallas_call(
        functools.partial(_prenorm_kernel, ctx_len=ctx_len, tm=tm),
        name="prenorm",
        grid=(b, t // tm),
        in_specs=[pl.BlockSpec((1, tm, d), lambda bi, i: (bi, i, 0)),
                  pl.BlockSpec((1, 2, 6, d), lambda bi, i: (bi, 0, 0, 0)),
                  pl.BlockSpec((4, d), lambda bi, i: (0, 0))],
        out_specs=pl.BlockSpec((1, tm, d), lambda bi, i: (bi, i, 0)),
        out_shape=jax.ShapeDtypeStruct((b, t, d), BF16),
        compiler_params=_params(("parallel", "parallel"), VMEM_LIMIT),
    )(h, mod, g4)
    tm = _row_tile(t, 1280)
    tn = _pick_tile(n, (1536, 1024, 512))
    return pl.pallas_call(
        _inproj_kernel,
        name="inproj",
        grid=(n // tn, b, t // tm),
        in_specs=[pl.BlockSpec((1, tm, d), lambda j, bi, i: (bi, i, 0)),
                  pl.BlockSpec((d, tn), lambda j, bi, i: (0, j))],
        out_specs=pl.BlockSpec((1, tm, tn), lambda j, bi, i: (bi, i, j)),
        out_shape=jax.ShapeDtypeStruct((b, t, n), F32),
        compiler_params=_params(("parallel", "parallel", "parallel"), VMEM_LIMIT),
    )(xn, w)


def _gdn_prep_kernel(u_ref, w_ref, o_ref, *, g, ctx_len):
    j = pl.program_id(1)
    x = u_ref[0]
    t = x.shape[0]
    row = lax.broadcasted_iota(jnp.int32, (t, 1), 0)
    first = (row == 0) | (row == ctx_len)
    last = (row == ctx_len - 1) | (row == t - 1)
    x_prev = jnp.where(first, 0.0, pltpu.roll(x, 1, 0))
    x_next = jnp.where(last, 0.0, pltpu.roll(x, t - 1, 0))
    y = x_prev * w_ref[0:1, :] + x * w_ref[1:2, :] + x_next * w_ref[2:3, :]
    y = y * _sigmoid(y)
    yn = y * lax.rsqrt(jnp.sum(y * y, axis=-1, keepdims=True) + NORM_EPS)
    scale = jnp.where(j < g, HEAD_DIM ** -0.5, 1.0)
    o_ref[0] = jnp.where(j < 2 * g, yn * scale, y)


def _gdn_prep(u, conv_w, g, ctx_len, off):
    b, t, _ = u.shape
    blk0 = off["gdn_q"] // LANE
    return pl.pallas_call(
        functools.partial(_gdn_prep_kernel, g=g, ctx_len=ctx_len),
        name="gdn_prep",
        grid=(b, 3 * g),
        in_specs=[pl.BlockSpec((1, t, LANE), lambda bi, j: (bi, 0, blk0 + j)),
                  pl.BlockSpec((3, LANE), lambda bi, j: (0, j))],
        out_specs=pl.BlockSpec((1, t, LANE), lambda bi, j: (bi, 0, j)),
        out_shape=jax.ShapeDtypeStruct((b, t, 3 * g * LANE), F32),
        compiler_params=_params(("parallel", "parallel"), VMEM_LIMIT),
    )(u, conv_w)


def _chunk_masks(fwd):
    ri = lax.broadcasted_iota(jnp.int32, (CHUNK, CHUNK), 0)
    ci = lax.broadcasted_iota(jnp.int32, (CHUNK, CHUNK), 1)
    if fwd:
        return ri >= ci, ri > ci, ri == ci
    return ri <= ci, ri < ci, ri == ci


def _bcast(col):
    return jnp.broadcast_to(col, (CHUNK, LANE))


def _split3(x):
    x1 = x.astype(BF16)
    r = x - x1.astype(F32)
    x2 = r.astype(BF16)
    return x1, x2, (r - x2.astype(F32)).astype(BF16)


def _chunk_cumsum(x, incl):
    lmat = jnp.where(incl, 1.0, 0.0).astype(BF16)
    p1, p2, p3 = _split3(x)
    return (jnp.dot(lmat, p1, preferred_element_type=F32) + jnp.dot(lmat, p2, preferred_element_type=F32)
            + jnp.dot(lmat, p3, preferred_element_type=F32))


def _pair_diff(g_b):
    return g_b[:, :CHUNK] - g_b.T[:CHUNK, :]


def _tri_inverse_all(a_list, diag):
    ri = lax.broadcasted_iota(jnp.int32, (CHUNK, CHUNK), 0)
    ci = lax.broadcasted_iota(jnp.int32, (CHUNK, CHUNK), 1)
    eye = diag.astype(F32)
    ts = [jnp.where((ri >> 1) == (ci >> 1), eye - a, 0.0) for a in a_list]
    for lb in range(1, int(math.log2(CHUNK))):
        join = ((ri >> (lb + 1)) == (ci >> (lb + 1))) & ((ri >> lb) != (ci >> lb))
        us = [_dot(jnp.where(join, a, 0.0), t) for a, t in zip(a_list, ts)]
        ts = [t - _dot(t, u) for t, u in zip(ts, us)]
    return ts


def _gdn_scan_kernel(qf_ref, gf_ref, qb_ref, gb_ref, al_ref, dt_ref, of_ref, ob_ref, s_ref, *, g, nb):
    @pl.when(pl.program_id(1) == 0)
    def _():
        s_ref[...] = jnp.zeros_like(s_ref)

    gw = g * HEAD_DIM
    streams = range(nb * 2 * g)
    fwd, incl, strict, q, k, v, g_b, beta_b, dest = [], [], [], [], [], [], [], [], []
    diag = _chunk_masks(True)[2]
    for bb in range(nb):
        for d, (q_ref, gt_ref) in enumerate(((qf_ref, gf_ref), (qb_ref, gb_ref))):
            m_incl, m_strict, _ = _chunk_masks(d == 0)
            gates = gt_ref[bb]
            la = -jnp.exp(al_ref[...]) * _softplus(gates + dt_ref[...])
            g_all = _chunk_cumsum(la, m_incl)
            be = _sigmoid(gates)
            for h in range(g):
                c = d * g + h
                fwd.append(d == 0)
                incl.append(m_incl)
                strict.append(m_strict)
                dest.append((of_ref if d == 0 else ob_ref, bb, h))
                q.append(q_ref[bb, :, h * HEAD_DIM:(h + 1) * HEAD_DIM])
                k.append(q_ref[bb, :, gw + h * HEAD_DIM:gw + (h + 1) * HEAD_DIM])
                v.append(q_ref[bb, :, 2 * gw + h * HEAD_DIM:2 * gw + (h + 1) * HEAD_DIM])
                g_b.append(_bcast(g_all[:, c:c + 1]))
                beta_b.append(_bcast(be[:, 2 * g + c:2 * g + c + 1]))

    decay = [jnp.exp(jnp.where(incl[i], _pair_diff(g_b[i]), -jnp.inf)) for i in streams]
    kb = [k[i] * beta_b[i] for i in streams]
    kq = [_dot_nt(jnp.concatenate([kb[i], q[i]], axis=0), k[i]) for i in streams]
    a = [jnp.where(strict[i], kq[i][:CHUNK] * decay[i], 0.0) for i in streams]
    attn = [kq[i][CHUNK:] * decay[i] for i in streams]
    t = _tri_inverse_all(a, diag)
    eg = [jnp.exp(g_b[i]) for i in streams]
    vw = [_dot(t[i], jnp.concatenate([v[i] * beta_b[i], kb[i] * eg[i]], axis=1)) for i in streams]
    s = [s_ref[i] for i in streams]
    ws = [_dot(jnp.concatenate([vw[i][:, HEAD_DIM:], q[i] * eg[i]], axis=0), s[i]) for i in streams]
    v_new = [vw[i][:, :HEAD_DIM] - ws[i][:CHUNK] for i in streams]
    o = [ws[i][CHUNK:] + _dot(attn[i], v_new[i]) for i in streams]
    g_end = [g_b[i][CHUNK - 1:CHUNK, :] if fwd[i] else g_b[i][0:1, :] for i in streams]
    s_new = [s[i] * jnp.exp(g_end[i]) + _dot_tn(k[i] * jnp.exp(g_end[i] - g_b[i]), v_new[i]) for i in streams]
    for i in streams:
        o_ref, bb, h = dest[i]
        o_ref[bb, :, h * HEAD_DIM:(h + 1) * HEAD_DIM] = o[i]
        s_ref[i] = s_new[i]


SCAN_BATCH = 8


def _scan_batch(b):
    return math.gcd(b, SCAN_BATCH)


def _scan_chunk_maps(n_ctx, n_all):
    def fwd(bi, s):
        return (bi, s, 0)

    def bwd(bi, s):
        return (bi, jnp.where(s < n_ctx, n_ctx - 1 - s, n_all + n_ctx - 1 - s), 0)

    return fwd, bwd


def _gdn_scan(qkv, u, a_log_row, dt_row, g, ctx_len, off):
    b, t, _ = qkv.shape
    n_all, n_ctx = t // CHUNK, ctx_len // CHUNK
    fwd, bwd = _scan_chunk_maps(n_ctx, n_all)
    gblk = off["gdn_ab"] // LANE

    def gate_map(m):
        return lambda bi, s: (m(bi, s)[0], m(bi, s)[1], gblk)

    gw = g * HEAD_DIM
    nb = _scan_batch(b)
    return pl.pallas_call(
        functools.partial(_gdn_scan_kernel, g=g, nb=nb),
        name="gdn_scan",
        grid=(b // nb, n_all),
        in_specs=[pl.BlockSpec((nb, CHUNK, 3 * gw), fwd),
                  pl.BlockSpec((nb, CHUNK, LANE), gate_map(fwd)),
                  pl.BlockSpec((nb, CHUNK, 3 * gw), bwd),
                  pl.BlockSpec((nb, CHUNK, LANE), gate_map(bwd)),
                  pl.BlockSpec((1, LANE), lambda bi, s: (0, 0)),
                  pl.BlockSpec((1, LANE), lambda bi, s: (0, 0))],
        out_specs=[pl.BlockSpec((nb, CHUNK, gw), fwd), pl.BlockSpec((nb, CHUNK, gw), bwd)],
        out_shape=[jax.ShapeDtypeStruct((b, t, gw), F32)] * 2,
        scratch_shapes=[pltpu.VMEM((nb * 2 * g, HEAD_DIM, HEAD_DIM), F32)],
        compiler_params=_params(("parallel", "arbitrary")),
    )(qkv, u, qkv, u, a_log_row, dt_row)


def _mlstm_scan_kernel(qf_ref, kf_ref, vf_ref, gf_ref, qb_ref, kb_ref, vb_ref, gb_ref, bias_ref,
                       of_ref, ob_ref, c_ref, n_ref, m_ref, *, g, nb):
    @pl.when(pl.program_id(1) == 0)
    def _():
        c_ref[...] = jnp.zeros_like(c_ref)
        n_ref[...] = jnp.zeros_like(n_ref)
        m_ref[...] = jnp.zeros_like(m_ref)

    streams = range(nb * 2 * g)
    fwd, incl, q, k, v, b_b, ip_b, dest = [], [], [], [], [], [], [], []
    dirs = ((qf_ref, kf_ref, vf_ref, gf_ref), (qb_ref, kb_ref, vb_ref, gb_ref))
    for bb in range(nb):
        for d, (q_ref, k_ref, v_ref, gt_ref) in enumerate(dirs):
            m_incl = _chunk_masks(d == 0)[0]
            gates = gt_ref[bb] + bias_ref[...]
            logsig = jnp.minimum(gates, 0.0) - jnp.log(1.0 + jnp.exp(-jnp.abs(gates)))
            b_all = _chunk_cumsum(logsig, m_incl)
            for h in range(g):
                c = d * g + h
                sl = slice(h * HEAD_DIM, (h + 1) * HEAD_DIM)
                fwd.append(d == 0)
                incl.append(m_incl)
                dest.append((of_ref if d == 0 else ob_ref, bb, h))
                q.append(q_ref[bb, :, sl] * (ML_DQK ** -0.5))
                k.append(k_ref[bb, :, sl])
                v.append(v_ref[bb, :, sl])
                b_b.append(_bcast(b_all[:, 2 * g + c:2 * g + c + 1]))
                ip_b.append(_bcast(gates[:, c:c + 1]))

    dmat = [jnp.where(incl[i], _pair_diff(b_b[i]) + ip_b[i].T[:CHUNK, :], -jnp.inf) for i in streams]
    qk = [_dot_nt(q[i], k[i]) for i in streams]
    cm = [c_ref[i] for i in streams]
    nrow = [n_ref[i] for i in streams]
    mrow = [m_ref[i] for i in streams]
    b_end = [b_b[i][CHUNK - 1:CHUNK, :] if fwd[i] else b_b[i][0:1, :] for i in streams]
    w_b = [b_end[i] - b_b[i] + ip_b[i] for i in streams]
    inter = [b_b[i] + mrow[i] for i in streams]
    m_t = [jnp.maximum(inter[i], jnp.max(dmat[i], axis=1, keepdims=True)) for i in streams]
    s_inter = [jnp.exp(inter[i] - m_t[i]) for i in streams]
    p = [jnp.exp(dmat[i] - m_t[i][:, :CHUNK]) * qk[i] for i in streams]
    num = [s_inter[i] * _dot(q[i], cm[i]) + _dot(p[i], v[i]) for i in streams]
    den = [s_inter[i] * jnp.sum(q[i] * nrow[i], axis=1, keepdims=True) + jnp.sum(p[i], axis=1, keepdims=True)
           for i in streams]
    hh = [num[i] / jnp.maximum(jnp.abs(den[i]), jnp.exp(-m_t[i])) for i in streams]
    m_new = [jnp.maximum(b_end[i] + mrow[i], jnp.max(w_b[i], axis=0, keepdims=True)) for i in streams]
    s_old = [jnp.exp(b_end[i] + mrow[i] - m_new[i]) for i in streams]
    wk = [jnp.exp(w_b[i] - m_new[i]) * k[i] for i in streams]
    cm_new = [s_old[i] * cm[i] + _dot_tn(wk[i], v[i]) for i in streams]
    n_new = [s_old[i] * nrow[i] + jnp.sum(wk[i], axis=0, keepdims=True) for i in streams]
    for i in streams:
        o_ref, bb, h = dest[i]
        o_ref[bb, :, h * HEAD_DIM:(h + 1) * HEAD_DIM] = hh[i]
        c_ref[i] = cm_new[i]
        n_ref[i] = n_new[i]
        m_ref[i] = m_new[i]


def _mlstm_scan(u, bias_row, g, ctx_len, off):
    b, t, _ = u.shape
    n_all, n_ctx = t // CHUNK, ctx_len // CHUNK
    fwd, bwd = _scan_chunk_maps(n_ctx, n_all)
    gw = g * HEAD_DIM

    def col_map(m, blk):
        return lambda bi, s: (m(bi, s)[0], m(bi, s)[1], blk)

    nb = _scan_batch(b)

    def specs(m):
        return [pl.BlockSpec((nb, CHUNK, gw), col_map(m, off["ml_q"] // gw)),
                pl.BlockSpec((nb, CHUNK, gw), col_map(m, off["ml_k"] // gw)),
                pl.BlockSpec((nb, CHUNK, gw), col_map(m, off["ml_v"] // gw)),
                pl.BlockSpec((nb, CHUNK, LANE), col_map(m, off["ml_if"] // LANE))]

    return pl.pallas_call(
        functools.partial(_mlstm_scan_kernel, g=g, nb=nb),
        name="mlstm_scan",
        grid=(b // nb, n_all),
        in_specs=specs(fwd) + specs(bwd) + [pl.BlockSpec((1, LANE), lambda bi, s: (0, 0))],
        out_specs=[pl.BlockSpec((nb, CHUNK, gw), fwd), pl.BlockSpec((nb, CHUNK, gw), bwd)],
        out_shape=[jax.ShapeDtypeStruct((b, t, gw), F32)] * 2,
        scratch_shapes=[pltpu.VMEM((nb * 2 * g, HEAD_DIM, HEAD_DIM), F32),
                        pltpu.VMEM((nb * 2 * g, 1, HEAD_DIM), F32),
                        pltpu.VMEM((nb * 2 * g, 1, HEAD_DIM), F32)],
        compiler_params=_params(("parallel", "arbitrary")),
    )(u, u, u, u, u, u, u, u, bias_row)


def _rope(x, c, s_up, s_dn, shift):
    return x * c + pltpu.roll(x, LANE - shift, 1) * s_up + pltpu.roll(x, shift, 1) * s_dn


def _gqa_prep_kernel(q_ref, k_ref, v_ref, c_ref, su_ref, sd_ref, n_ref, qo_ref, ko_ref, vo_ref, *, g):
    c, su, sd = c_ref[...], su_ref[...], sd_ref[...]
    scale = HEAD_DIM ** -0.5 * LOG2_E
    for h in range(g):
        sl = slice(h * HEAD_DIM, (h + 1) * HEAD_DIM)
        qo_ref[0, :, sl] = (_rope(_rms(q_ref[0, :, sl], n_ref[0:1, :]), c, su, sd, 32) * scale).astype(BF16)
    for h in range(g // 2):
        sl = slice(h * HEAD_DIM, (h + 1) * HEAD_DIM)
        ko_ref[0, :, sl] = _rope(_rms(k_ref[0, :, sl], n_ref[1:2, :]), c, su, sd, 32).astype(BF16)
    vo_ref[0] = v_ref[0].astype(BF16)


def _gqa_prep(u, tables, qk_norm, g, off):
    b, t, _ = u.shape
    tm = _row_tile(t)
    gw, kw = g * HEAD_DIM, g * HEAD_DIM // 2
    tab = pl.BlockSpec((tm, LANE), lambda bi, i: (i, 0))
    return pl.pallas_call(
        functools.partial(_gqa_prep_kernel, g=g),
        name="gqa_prep",
        grid=(b, t // tm),
        in_specs=[pl.BlockSpec((1, tm, gw), lambda bi, i: (bi, i, off["gqa_q"] // gw)),
                  pl.BlockSpec((1, tm, kw), lambda bi, i: (bi, i, off["gqa_k"] // kw)),
                  pl.BlockSpec((1, tm, kw), lambda bi, i: (bi, i, off["gqa_v"] // kw)),
                  tab, tab, tab,
                  pl.BlockSpec((2, HEAD_DIM), lambda bi, i: (0, 0))],
        out_specs=[pl.BlockSpec((1, tm, gw), lambda bi, i: (bi, i, 0)),
                   pl.BlockSpec((1, tm, kw), lambda bi, i: (bi, i, 0)),
                   pl.BlockSpec((1, tm, kw), lambda bi, i: (bi, i, 0))],
        out_shape=[jax.ShapeDtypeStruct((b, t, gw), BF16),
                   jax.ShapeDtypeStruct((b, t, kw), BF16),
                   jax.ShapeDtypeStruct((b, t, kw), BF16)],
        compiler_params=_params(("parallel", "parallel")),
    )(u, u, u, *tables, qk_norm)


def _mla_prep_kernel(x_ref, kpe_ref, c_ref, su_ref, sd_ref, qn_ref, kvn_ref, wq_ref, wk_ref, wv_ref,
                     qo_ref, ko_ref, vo_ref, *, g):
    c, su, sd = c_ref[...], su_ref[...], sd_ref[...]
    scale = (MLA_NOPE + MLA_ROPE) ** -0.5 * LOG2_E
    x = x_ref[0]
    cq = _rms(x[:, :MLA_Q_RANK], qn_ref[...])
    ckv = _rms(x[:, MLA_Q_RANK:], kvn_ref[...])
    q = _dot(cq, wq_ref[...])
    kn = _dot(ckv, wk_ref[...])
    vo_ref[0] = _dot(ckv, wv_ref[...]).astype(BF16)
    kpe = _rope(kpe_ref[0], c, su, sd, 16).astype(BF16)
    for h in range(g):
        lo = h * MLA_QK_PAD
        qo_ref[0, :, lo:lo + LANE] = (q[:, lo:lo + LANE] * scale).astype(BF16)
        qo_ref[0, :, lo + LANE:lo + 2 * LANE] = (
            _rope(q[:, lo + LANE:lo + 2 * LANE], c, su, sd, 16) * scale).astype(BF16)
        ko_ref[0, :, lo:lo + LANE] = kn[:, h * LANE:(h + 1) * LANE].astype(BF16)
        ko_ref[0, :, lo + LANE:lo + 2 * LANE] = kpe


def _mla_prep(u, tables, q_norm, kv_norm, wq, wk, wv, g, off):
    b, t, _ = u.shape
    tm = _row_tile(t)
    cw = MLA_Q_RANK + MLA_KV_RANK
    tab = pl.BlockSpec((tm, LANE), lambda bi, i: (i, 0))

    def full(a):
        return pl.BlockSpec(a.shape, lambda bi, i: (0,) * a.ndim)

    return pl.pallas_call(
        functools.partial(_mla_prep_kernel, g=g),
        name="mla_prep",
        grid=(b, t // tm),
        in_specs=[pl.BlockSpec((1, tm, cw), lambda bi, i: (bi, i, off["mla_cqkv"] // cw)),
                  pl.BlockSpec((1, tm, LANE), lambda bi, i: (bi, i, off["mla_kpe"] // LANE)),
                  tab, tab, tab, full(q_norm), full(kv_norm), full(wq), full(wk), full(wv)],
        out_specs=[pl.BlockSpec((1, tm, g * MLA_QK_PAD), lambda bi, i: (bi, i, 0)),
                   pl.BlockSpec((1, tm, g * MLA_QK_PAD), lambda bi, i: (bi, i, 0)),
                   pl.BlockSpec((1, tm, g * MLA_DV), lambda bi, i: (bi, i, 0))],
        out_shape=[jax.ShapeDtypeStruct((b, t, g * MLA_QK_PAD), BF16),
                   jax.ShapeDtypeStruct((b, t, g * MLA_QK_PAD), BF16),
                   jax.ShapeDtypeStruct((b, t, g * MLA_DV), BF16)],
        compiler_params=_params(("parallel", "parallel")),
    )(u, u, *tables, q_norm, kv_norm, wq, wk, wv)


KV_HEADS_PER_STEP = 4


def _attn_kernel(q_ref, k_ref, v_ref, o_ref, *, ctx_len, tq, hp, rep, dq, dv):
    heads = range(hp)

    def attend(nk):
        s = [lax.dot_general(q_ref[0, :, j * dq:(j + 1) * dq], k_ref[0, :nk, (j // rep) * dq:(j // rep + 1) * dq],
                             (((1,), (1,)), ((), ())), preferred_element_type=F32) for j in heads]
        p = [jnp.exp2(s[j] - jnp.max(s[j], axis=-1, keepdims=True)) for j in heads]
        l = [jnp.sum(p[j], axis=-1, keepdims=True) for j in heads]
        o = [jnp.dot(p[j].astype(BF16), v_ref[0, :nk, (j // rep) * dv:(j // rep + 1) * dv],
                     preferred_element_type=F32) for j in heads]
        for j in heads:
            o_ref[0, :, j * dv:(j + 1) * dv] = (o[j] / l[j]).astype(o_ref.dtype)

    is_ctx = pl.program_id(2) * tq < ctx_len

    @pl.when(is_ctx)
    def _():
        attend(ctx_len)

    @pl.when(jnp.logical_not(is_ctx))
    def _():
        attend(k_ref.shape[1])


def _attention(q, k, v, heads, kv_heads, ctx_len):
    b, t, _ = q.shape
    dq, dv = q.shape[2] // heads, v.shape[2] // kv_heads
    rep = heads // kv_heads
    kvp = math.gcd(kv_heads, KV_HEADS_PER_STEP)
    hp = kvp * rep
    tq = math.gcd(256, ctx_len)
    return pl.pallas_call(
        functools.partial(_attn_kernel, ctx_len=ctx_len, tq=tq, hp=hp, rep=rep, dq=dq, dv=dv),
        name="attn",
        grid=(b, heads // hp, t // tq),
        in_specs=[pl.BlockSpec((1, tq, hp * dq), lambda bi, h, i: (bi, i, h)),
                  pl.BlockSpec((1, t, kvp * dq), lambda bi, h, i: (bi, 0, h * hp // (rep * kvp))),
                  pl.BlockSpec((1, t, kvp * dv), lambda bi, h, i: (bi, 0, h * hp // (rep * kvp)))],
        out_specs=pl.BlockSpec((1, tq, hp * dv), lambda bi, h, i: (bi, i, h)),
        out_shape=jax.ShapeDtypeStruct((b, t, heads * dv), BF16),
        compiler_params=_params(("parallel", "parallel", "parallel"), VMEM_LIMIT),
    )(q, k, v)


def _head_norm(x, w_row, g):
    parts = []
    for h in range(g):
        xs = x[:, h * HEAD_DIM:(h + 1) * HEAD_DIM]
        parts.append(xs * lax.rsqrt(jnp.mean(xs * xs, axis=-1, keepdims=True) + NORM_EPS))
    return jnp.concatenate(parts, axis=-1) * w_row


def _outproj_kernel(af_ref, ab_ref, ga_ref, yb_ref, cf_ref, cb_ref, gc_ref, yd_ref, h_ref, mod_ref, g_ref,
                    na_ref, nc_ref, w_ref, o_ref, *, g, ctx_len, tm):
    ga = ga_ref[0]
    ya = _head_norm(af_ref[0] + ab_ref[0], na_ref[...], g) * (ga * _sigmoid(ga))
    yc = _head_norm(cf_ref[0] + cb_ref[0], nc_ref[...], g) * _sigmoid(gc_ref[0])
    y = jnp.concatenate([ya.astype(BF16), yb_ref[0], yc.astype(BF16), yd_ref[0]], axis=-1)
    z = jnp.dot(y, w_ref[...], preferred_element_type=F32)

    def residual(row):
        gain = g_ref[1:2, :] * _mod_rows(mod_ref, 2, row, ctx_len)
        o_ref[0] = h_ref[0] + _unit_rms(z) * gain

    _per_tile_kind(tm, ctx_len, residual)


def _outproj(oa, ob, oc, od, u, h, mod, g4, na_row, nc_row, w, g, ctx_len, off):
    b, t, d = h.shape
    tm = _row_tile(t)
    gw = g * HEAD_DIM

    def rows(width, blk=0):
        return pl.BlockSpec((1, tm, width), lambda bi, i: (bi, i, blk))

    return pl.pallas_call(
        functools.partial(_outproj_kernel, g=g, ctx_len=ctx_len, tm=tm),
        name="outproj",
        grid=(b, t // tm),
        in_specs=[rows(gw), rows(gw), rows(gw, off["gdn_gate"] // gw), rows(gw),
                  rows(gw), rows(gw), rows(gw, off["ml_og"] // gw), rows(gw),
                  rows(d),
                  pl.BlockSpec((1, 2, 6, d), lambda bi, i: (bi, 0, 0, 0)),
                  pl.BlockSpec((4, d), lambda bi, i: (0, 0)),
                  pl.BlockSpec((1, gw), lambda bi, i: (0, 0)),
                  pl.BlockSpec((1, gw), lambda bi, i: (0, 0)),
                  pl.BlockSpec(w.shape, lambda bi, i: (0, 0))],
        out_specs=rows(d),
        out_shape=jax.ShapeDtypeStruct((b, t, d), F32),
        compiler_params=_params(("parallel", "parallel"), VMEM_LIMIT),
    )(oa[0], oa[1], u, ob, oc[0], oc[1], u, od, h, mod, g4, na_row, nc_row, w)


FFN_ROW_PIECES = 2


def _ffn_kernel(h_ref, mod_ref, g_ref, w1_ref, w2_ref, o_ref, xn_ref, acc_ref, *, ctx_len, tm):
    k = pl.program_id(2)
    last = pl.num_programs(2) - 1
    pieces = [slice(lo, lo + tm // FFN_ROW_PIECES) for lo in range(0, tm, tm // FFN_ROW_PIECES)]

    def mlp(rows):
        hid = jnp.maximum(jnp.dot(xn_ref[rows, :], w1_ref[...], preferred_element_type=F32), 0.0)
        return jnp.dot((hid * hid).astype(BF16), w2_ref[...], preferred_element_type=F32)

    def piece_rows(row, rows):
        return None if row is None else row[rows]

    def first_step(row):
        for rows in pieces:
            r = piece_rows(row, rows)
            gain = g_ref[2:3, :] * (1.0 + _mod_rows(mod_ref, 4, r, ctx_len))
            xn_ref[rows, :] = (_unit_rms(h_ref[0, rows, :]) * gain + _mod_rows(mod_ref, 3, r, ctx_len)).astype(BF16)
            acc_ref[rows, :] = mlp(rows)

    def last_step(row):
        for rows in pieces:
            gain = g_ref[3:4, :] * _mod_rows(mod_ref, 5, piece_rows(row, rows), ctx_len)
            o_ref[0, rows, :] = h_ref[0, rows, :] + _unit_rms(acc_ref[rows, :] + mlp(rows)) * gain

    _per_tile_kind(tm, ctx_len, first_step, also=k == 0)

    @pl.when(jnp.logical_and(k > 0, k < last))
    def _():
        acc_ref[...] += mlp(slice(None))

    _per_tile_kind(tm, ctx_len, last_step, also=k == last)


def _ffn(h, mod, g4, w1, w2, ctx_len):
    b, t, d = h.shape
    ff = w1.shape[1]
    tm = _row_tile(t)
    tf = _pick_tile(ff, (1024, 512, 256, 128))
    assert ff // tf >= 2 and tm % (16 * FFN_ROW_PIECES) == 0
    return pl.pallas_call(
        functools.partial(_ffn_kernel, ctx_len=ctx_len, tm=tm),
        name="ffn",
        grid=(b, t // tm, ff // tf),
        in_specs=[pl.BlockSpec((1, tm, d), lambda bi, i, k: (bi, i, 0)),
                  pl.BlockSpec((1, 2, 6, d), lambda bi, i, k: (bi, 0, 0, 0)),
                  pl.BlockSpec((4, d), lambda bi, i, k: (0, 0)),
                  pl.BlockSpec((d, tf), lambda bi, i, k: (0, k)),
                  pl.BlockSpec((tf, d), lambda bi, i, k: (k, 0))],
        out_specs=pl.BlockSpec((1, tm, d), lambda bi, i, k: (bi, i, 0)),
        out_shape=jax.ShapeDtypeStruct((b, t, d), F32),
        scratch_shapes=[pltpu.VMEM((tm, d), BF16), pltpu.VMEM((tm, d), F32)],
        compiler_params=_params(("parallel", "parallel", "arbitrary"), VMEM_LIMIT),
    )(h, mod, g4, w1, w2)


def _rope_tables(seq, ctx_len, dim):
    quarter = dim // 4
    t = jnp.arange(seq, dtype=jnp.int32)
    inv_freq = ROPE_BASE ** (-jnp.arange(quarter, dtype=F32) / quarter)
    ang_r = (t // GRID_W).astype(F32)[:, None] * inv_freq
    ang_c = (t % GRID_W).astype(F32)[:, None] * inv_freq
    zero = jnp.zeros_like(ang_r)
    cos = jnp.concatenate([jnp.cos(ang_r)] * 2 + [jnp.cos(ang_c)] * 2, axis=-1)
    s_up = jnp.concatenate([-jnp.sin(ang_r), zero, -jnp.sin(ang_c), zero], axis=-1)
    s_dn = jnp.concatenate([zero, jnp.sin(ang_r), zero, jnp.sin(ang_c)], axis=-1)

    def finish(tab, fill):
        tab = jnp.pad(tab, ((0, 0), (0, LANE - dim)), constant_values=fill)
        return jnp.pad(tab, ((ctx_len, 0), (0, 0)), constant_values=fill)

    return finish(cos, 1.0), finish(s_up, 0.0), finish(s_dn, 0.0)


def _lane_row(v):
    v = v.reshape(1, -1)
    return jnp.pad(v, ((0, 0), (0, LANE - v.shape[1])))


def kernel(x, c, ctx, c_ctx, w_mod, b_mod, g_norm, w_in, gdn_conv, gdn_a_log, gdn_dt_bias, gdn_norm,
           gqa_qk_norm, mlstm_gate_bias, mlstm_norm, mla_q_norm, mla_kv_norm, mla_w_qb, mla_w_kvb,
           w_out, w_ff1, w_ff2):
    bsz, seq, d = x.shape
    ctx_len = ctx.shape[1]
    depth = w_in.shape[0]
    g = w_out.shape[1] // (4 * HEAD_DIM)
    assert ctx_len % CHUNK == 0 and seq % CHUNK == 0 and seq % GRID_W == 0
    off, _ = _layout(g)
    cols, n_pad, n_src = _packed_columns(g)
    assert n_src == w_in.shape[2]

    w_in_p = _pack_columns(w_in.astype(BF16), cols)
    wq = mla_w_qb.reshape(depth, MLA_Q_RANK, g, MLA_NOPE + MLA_ROPE)
    wq = jnp.pad(wq, ((0, 0), (0, 0), (0, 0), (0, MLA_QK_PAD - MLA_NOPE - MLA_ROPE)))
    wq = wq.reshape(depth, MLA_Q_RANK, g * MLA_QK_PAD).astype(BF16)
    wkv = mla_w_kvb.reshape(depth, MLA_KV_RANK, g, MLA_NOPE + MLA_DV)
    wk = wkv[..., :MLA_NOPE].reshape(depth, MLA_KV_RANK, g * MLA_NOPE).astype(BF16)
    wv = wkv[..., MLA_NOPE:].reshape(depth, MLA_KV_RANK, g * MLA_DV).astype(BF16)
    w_out_b, w_ff1_b, w_ff2_b = w_out.astype(BF16), w_ff1.astype(BF16), w_ff2.astype(BF16)
    gqa_tab = _rope_tables(seq, ctx_len, HEAD_DIM)
    mla_tab = _rope_tables(seq, ctx_len, MLA_ROPE)

    n_rows = -(-(bsz + 1) // 8) * 8
    c_rows = jnp.concatenate([c, c_ctx[None, :], jnp.zeros((n_rows - bsz - 1, d), F32)], axis=0)
    mod_all = _mod_all(c_rows, w_mod, b_mod).reshape(depth, n_rows, 6, d)
    mod_all = jnp.stack([jnp.broadcast_to(mod_all[:, bsz:bsz + 1], (depth, bsz, 6, d)), mod_all[:, :bsz]], axis=2)

    h = jnp.concatenate([ctx, x], axis=1)
    for l in range(depth):
        mod, g4 = mod_all[l], g_norm[l]
        u = _inproj(h, mod, g4, w_in_p[l], ctx_len)
        qkv = _gdn_prep(u, gdn_conv[l], g, ctx_len, off)
        oa = _gdn_scan(qkv, u, _lane_row(gdn_a_log[l]), _lane_row(gdn_dt_bias[l]), g, ctx_len, off)
        qn, kn, vn = _gqa_prep(u, gqa_tab, gqa_qk_norm[l], g, off)
        ob = _attention(qn, kn, vn, g, g // 2, ctx_len)
        oc = _mlstm_scan(u, _lane_row(mlstm_gate_bias[l]), g, ctx_len, off)
        qm, km, vm = _mla_prep(u, mla_tab, mla_q_norm[l].reshape(1, -1), mla_kv_norm[l].reshape(1, -1),
                               wq[l], wk[l], wv[l], g, off)
        od = _attention(qm, km, vm, g, g, ctx_len)
        h = _outproj(oa, ob, oc, od, u, h, mod, g4, jnp.tile(gdn_norm[l], g).reshape(1, -1),
                     mlstm_norm[l].reshape(1, -1), w_out_b[l], g, ctx_len, off)
        h = _ffn(h, mod, g4, w_ff1_b[l], w_ff2_b[l], ctx_len)
    return h[:, ctx_len:, :]
```
